```python
import jax, jax.numpy as jnp
from jax import lax
import numpy as np

D_MODEL = 1024
BATCH = 4
SEQ = 8192
DEPTH = 2
DEC_BATCH = 32
DEC_SEQ = 4
PAST_LEN = 16384
PAGE_SIZE = 128

HEAD_DIM = 64
D_B = D_MODEL // 4
D_A = (D_MODEL - D_B) // 2
D_C = D_MODEL - D_B - D_A
H_A = D_A // HEAD_DIM
H_C = D_C // HEAD_DIM
KVH_C = 2
GROUP_C = H_C // KVH_C
MOBA_BLOCK = 256
MOBA_TOPK = 3
CMP_STRIDE = 16
CMP_BLK = 2 * CMP_STRIDE
SLC_BLK = 64
SLC_TOPK = 16
WINDOW = 512
CONV_W = 31
D_FF = 4 * D_MODEL
Q_BLOCK = 32
PAD_UNIT = 256
EPS = 1e-6
NEG = -1e30
FORCE_SCORE = 1e4
SCALE = HEAD_DIM ** -0.5
D_KV_C = KVH_C * HEAD_DIM
OFF_QA = 0
OFF_KA = OFF_QA + D_A
OFF_VA = OFF_KA + D_A
OFF_GLU = OFF_VA + D_A
OFF_QC = OFF_GLU + 2 * D_B
OFF_KVC = OFF_QC + D_C
OFF_GATE = OFF_KVC + 6 * D_KV_C
N_IN = OFF_GATE + 3 * H_C

kernel_name = "hymba_moba_conformer_nsa_decode_step"


def alibi_slopes(n):
    return jnp.asarray(np.array([2.0 ** (-8.0 * (i + 1) / n) for i in range(n)], dtype=np.float32))


def rms_norm(x, g):
    xf = x.astype(jnp.float32)
    y = xf * lax.rsqrt(jnp.mean(xf * xf, axis=-1, keepdims=True) + EPS)
    return (y * g.astype(jnp.float32)).astype(x.dtype)


def layer_norm(x, g, b):
    xf = x.astype(jnp.float32)
    mu = jnp.mean(xf, axis=-1, keepdims=True)
    var = jnp.mean(jnp.square(xf - mu), axis=-1, keepdims=True)
    return ((xf - mu) * lax.rsqrt(var + EPS) * g.astype(jnp.float32) + b.astype(jnp.float32)).astype(x.dtype)


def masked_softmax(s, mask):
    p = jax.nn.softmax(jnp.where(mask, s, NEG), axis=-1)
    return jnp.where(mask, p, 0.0)


def nsa_compress(raw, pos, w1, w2):
    b, lp, kvh, d = raw.shape
    ch = raw.reshape(b, lp // CMP_STRIDE, CMP_STRIDE, kvh, d)
    blk = jnp.concatenate([ch[:, :-1], ch[:, 1:]], axis=2) + pos[None, None, :, None, :]
    blk = blk.transpose(0, 1, 3, 2, 4).reshape(b, lp // CMP_STRIDE - 1, kvh, CMP_BLK * d)
    return jax.nn.gelu(blk @ w1) @ w2


def moba_chunk(q, t, kbt, vbt, kmean, slopes):
    b, nq, h, d = q.shape
    nb = kbt.shape[2]
    own = t // MOBA_BLOCK
    gate = jnp.einsum('bqhd,bnhd->bqhn', q, kmean).astype(jnp.float32)
    past = jnp.arange(nb)[None, :] < own[:, None]
    gate = jnp.where(past[None, :, None, :], gate, -jnp.inf)
    top_val, top_idx = lax.top_k(gate, min(MOBA_TOPK, nb))
    idx = jnp.concatenate([top_idx.astype(jnp.int32), jnp.broadcast_to(own[None, :, None, None], (b, nq, h, 1))], axis=-1)
    ok = jnp.concatenate([top_val > -jnp.inf, jnp.ones((b, nq, h, 1), dtype=bool)], axis=-1)
    bi = jnp.arange(b)[:, None, None, None]
    hi = jnp.arange(h)[None, None, :, None]
    kg = kbt[bi, hi, idx]
    vg = vbt[bi, hi, idx]
    pos = idx[..., None] * MOBA_BLOCK + jnp.arange(MOBA_BLOCK, dtype=jnp.int32)
    tq = t[None, :, None, None, None]
    mask = ok[..., None] & (pos <= tq)
    s = jnp.einsum('bqhd,bqhkjd->bqhkj', q, kg).astype(jnp.float32) * SCALE \
        - slopes[None, None, :, None, None] * (tq - pos).astype(jnp.float32)
    p = masked_softmax(s.reshape(b, nq, h, -1), mask.reshape(b, nq, h, -1))
    return jnp.einsum('bqhn,bqhnd->bqhd', p.astype(vg.dtype), vg.reshape(b, nq, h, -1, d))


def nsa_chunk(q, t, gate_logits, kc, vc, kslc_t, vslc_t, overlap, kw, vw, posw, validw, slopes):
    b, nq, h, d = q.shape
    qg = q.reshape(b, nq, KVH_C, GROUP_C, d)
    m = slopes.reshape(KVH_C, GROUP_C)[None, None, :, :, None]
    tq = t[None, :, None, None, None]
    n_cmp = kc.shape[1]
    cmp_end = jnp.arange(n_cmp, dtype=jnp.int32) * CMP_STRIDE + (CMP_BLK - 1)
    s = jnp.einsum('bqgrd,bngd->bqgrn', qg, kc).astype(jnp.float32) * SCALE - m * (tq - cmp_end).astype(jnp.float32)
    mask_c = (cmp_end[None, :] <= t[:, None])[None, :, None, None, :]
    p_cmp = masked_softmax(s, mask_c)
    o_cmp = jnp.einsum('bqgrn,bngd->bqgrd', p_cmp.astype(vc.dtype), vc)
    imp = jnp.einsum('bqgrn,nm->bqgm', p_cmp, overlap)
    n_slc = overlap.shape[1]
    j = jnp.arange(n_slc, dtype=jnp.int32)[None, :]
    own = (t // SLC_BLK)[:, None]
    forced = (j == 0) | (j == own) | (j == own - 1)
    imp = jnp.where(forced[None, :, None, :], FORCE_SCORE, jnp.where((j <= own)[None, :, None, :], imp, -jnp.inf))
    top_val, idx = lax.top_k(imp, min(SLC_TOPK, n_slc))
    idx = idx.astype(jnp.int32)
    ok = top_val > -jnp.inf
    bi = jnp.arange(b)[:, None, None, None]
    gi = jnp.arange(KVH_C)[None, None, :, None]
    kg = kslc_t[bi, gi, idx]
    vg = vslc_t[bi, gi, idx]
    pos = idx[..., None] * SLC_BLK + jnp.arange(SLC_BLK, dtype=jnp.int32)
    mask_s = ok[..., None] & (pos <= t[None, :, None, None, None])
    dist = (t[None, :, None, None, None] - pos).astype(jnp.float32)
    s = jnp.einsum('bqgrd,bqgkjd->bqgrkj', qg, kg).astype(jnp.float32) * SCALE - m[..., None] * dist[:, :, :, None]
    p = masked_softmax(s.reshape(b, nq, KVH_C, GROUP_C, -1), mask_s.reshape(b, nq, KVH_C, 1, -1))
    o_slc = jnp.einsum('bqgrn,bqgnd->bqgrd', p.astype(vg.dtype), vg.reshape(b, nq, KVH_C, -1, d))
    s = jnp.einsum('bqgrd,bsgd->bqgrs', qg, kw).astype(jnp.float32) * SCALE - m * (tq - posw).astype(jnp.float32)
    mask_w = validw[None, :] & (posw[None, :] <= t[:, None]) & (t[:, None] - posw[None, :] <= WINDOW)
    p = masked_softmax(s, mask_w[None, :, None, None, :])
    o_win = jnp.einsum('bqgrs,bsgd->bqgrd', p.astype(vw.dtype), vw)
    g = jax.nn.sigmoid(gate_logits.astype(jnp.float32)).reshape(b, nq, KVH_C, GROUP_C, 3)
    o = g[..., 0:1] * o_cmp.astype(jnp.float32) + g[..., 1:2] * o_slc.astype(jnp.float32) \
        + g[..., 2:3] * o_win.astype(jnp.float32)
    return o.astype(q.dtype).reshape(b, nq, h, d)


def trunk_layer(x, past_kv_a, past_kv_c, win_buf, conv_buf, win_keep,
                ln1_g, w_in, qk_gain, cmp_pos, cmp_w1, cmp_w2, conv_w, conv_b, conv_ln_g, conv_ln_b,
                w_out, ln2_g, w_up, w_down):
    b, T, _ = x.shape
    past_len = past_kv_a.shape[1]
    L = past_len + T
    l_pad = -(-L // PAD_UNIT) * PAD_UNIT
    h = rms_norm(x, ln1_g)
    z = h @ w_in
    qa = rms_norm(z[..., OFF_QA:OFF_KA].reshape(b, T, H_A, HEAD_DIM), qk_gain[0])
    ka = rms_norm(z[..., OFF_KA:OFF_VA].reshape(b, T, H_A, HEAD_DIM), qk_gain[1])
    va = z[..., OFF_VA:OFF_GLU].reshape(b, T, H_A, HEAD_DIM)
    glu = z[..., OFF_GLU:OFF_GLU + D_B] * jax.nn.sigmoid(z[..., OFF_GLU + D_B:OFF_QC])
    qc = rms_norm(z[..., OFF_QC:OFF_KVC].reshape(b, T, H_C, HEAD_DIM), qk_gain[2])
    kvc = z[..., OFF_KVC:OFF_GATE].reshape(b, T, 6, KVH_C, HEAD_DIM)
    gate_logits = z[..., OFF_GATE:].reshape(b, T, H_C, 3)
    k_slc = rms_norm(kvc[:, :, 2], qk_gain[4])
    k_win = rms_norm(kvc[:, :, 4], qk_gain[5])
    new_kv_a = jnp.stack([ka, va], axis=2)
    new_kv_c = jnp.stack([kvc[:, :, 0], kvc[:, :, 1], k_slc, kvc[:, :, 3]], axis=2)
    new_win = jnp.stack([k_win, kvc[:, :, 5]], axis=2)

    full_a = jnp.pad(jnp.concatenate([past_kv_a, new_kv_a], axis=1), ((0, 0), (0, l_pad - L), (0, 0), (0, 0), (0, 0)))
    nb = l_pad // MOBA_BLOCK
    kbt = full_a[:, :, 0].reshape(b, nb, MOBA_BLOCK, H_A, HEAD_DIM).transpose(0, 3, 1, 2, 4)
    vbt = full_a[:, :, 1].reshape(b, nb, MOBA_BLOCK, H_A, HEAD_DIM).transpose(0, 3, 1, 2, 4)
    kmean = jnp.mean(kbt.astype(jnp.float32), axis=3).astype(kbt.dtype).transpose(0, 2, 1, 3)

    full_c = jnp.pad(jnp.concatenate([past_kv_c, new_kv_c], axis=1), ((0, 0), (0, l_pad - L), (0, 0), (0, 0), (0, 0)))
    kc = rms_norm(nsa_compress(full_c[:, :, 0], cmp_pos[0], cmp_w1[0], cmp_w2[0]), qk_gain[3])
    vc = nsa_compress(full_c[:, :, 1], cmp_pos[1], cmp_w1[1], cmp_w2[1])
    n_slc = l_pad // SLC_BLK
    kslc_t = full_c[:, :, 2].reshape(b, n_slc, SLC_BLK, KVH_C, HEAD_DIM).transpose(0, 3, 1, 2, 4)
    vslc_t = full_c[:, :, 3].reshape(b, n_slc, SLC_BLK, KVH_C, HEAD_DIM).transpose(0, 3, 1, 2, 4)
    n_cmp = kc.shape[1]
    start = jnp.arange(n_cmp, dtype=jnp.int32) * CMP_STRIDE
    jj = jnp.arange(n_slc, dtype=jnp.int32)[None, :]
    overlap = ((jj == (start // SLC_BLK)[:, None]) | (jj == ((start + CMP_BLK - 1) // SLC_BLK)[:, None])).astype(jnp.float32)

    nbuf = win_buf.shape[1]
    ctx = jnp.pad(win_buf, ((0, 0), (WINDOW - nbuf, 0), (0, 0), (0, 0), (0, 0)))
    kw_all = jnp.concatenate([ctx, new_win], axis=1)
    posw = past_len - WINDOW + jnp.arange(WINDOW + T, dtype=jnp.int32)
    validw = jnp.arange(WINDOW + T) >= WINDOW - nbuf
    new_win_state = jnp.concatenate([win_buf, new_win], axis=1)[:, -win_keep:]

    qb = Q_BLOCK if T % Q_BLOCK == 0 else T
    n_ch = T // qb
    t_all = past_len + jnp.arange(T, dtype=jnp.int32)
    chunked = lambda a: a.reshape((b, n_ch, qb) + a.shape[2:]).swapaxes(0, 1)
    slopes_a = alibi_slopes(H_A)
    slopes_c = alibi_slopes(H_C)

    def per_chunk(args):
        qa_c, qc_c, gl_c, t_c, c = args
        o_a = moba_chunk(qa_c, t_c, kbt, vbt, kmean, slopes_a)
        kw = lax.dynamic_slice_in_dim(kw_all, c * qb, WINDOW + qb, axis=1)
        pw = lax.dynamic_slice_in_dim(posw, c * qb, WINDOW + qb)
        vld = lax.dynamic_slice_in_dim(validw, c * qb, WINDOW + qb)
        o_c = nsa_chunk(qc_c, t_c, gl_c, kc, vc, kslc_t, vslc_t, overlap, kw[:, :, 0], kw[:, :, 1], pw, vld, slopes_c)
        return o_a, o_c

    o_a, o_c = lax.map(per_chunk, (chunked(qa), chunked(qc), chunked(gate_logits),
                                   t_all.reshape(n_ch, qb), jnp.arange(n_ch, dtype=jnp.int32)))
    o_a = o_a.swapaxes(0, 1).reshape(b, T, D_A)
    o_c = o_c.swapaxes(0, 1).reshape(b, T, D_C)

    conv_in = jnp.concatenate([conv_buf, glu], axis=1)
    y = lax.conv_general_dilated(conv_in, conv_w[:, None, :], (1,), 'VALID',
                                 dimension_numbers=('NWC', 'WIO', 'NWC'), feature_group_count=D_B) + conv_b
    y = jax.nn.silu(layer_norm(y, conv_ln_g, conv_ln_b))
    new_conv = conv_in[:, -(CONV_W - 1):]

    x = x + jnp.concatenate([o_a, y, o_c], axis=-1) @ w_out
    h2 = rms_norm(x, ln2_g)
    x = x + jnp.square(jax.nn.relu(h2 @ w_up)) @ w_down
    return x, new_kv_a, new_kv_c, new_win_state, new_conv


def setup_inputs(seed: int = 0) -> dict:
    key = jax.random.key(seed)
    ks = jax.random.split(key, 24)
    n_pages = PAST_LEN // PAGE_SIZE
    n_pool = (DEC_BATCH * n_pages * 5) // 4
    win_s = min(WINDOW, PAST_LEN)
    nrm = lambda k, shape, scale=1.0: scale * jax.random.normal(k, shape, jnp.float32)
    page_table = jax.random.permutation(ks[6], n_pool)[:DEC_BATCH * n_pages].reshape(DEC_BATCH, n_pages).astype(jnp.int32)
    return {
        'x_prompt': nrm(ks[0], (BATCH, SEQ, D_MODEL)),
        'x_sample': nrm(ks[1], (DEC_BATCH, DEC_SEQ, D_MODEL)),
        'cache_moba_kv': nrm(ks[2], (DEPTH, n_pool, PAGE_SIZE, 2, H_A, HEAD_DIM)),
        'cache_nsa_kv': nrm(ks[3], (DEPTH, n_pool, PAGE_SIZE, 4, KVH_C, HEAD_DIM)),
        'cache_win_kv': nrm(ks[4], (DEPTH, DEC_BATCH, win_s, 2, KVH_C, HEAD_DIM)),
        'cache_conv': nrm(ks[5], (DEPTH, DEC_BATCH, CONV_W - 1, D_B), 0.5),
        'page_table': page_table,
        'ln1_g': 1.0 + nrm(ks[7], (DEPTH, D_MODEL), 0.1),
        'w_in': nrm(ks[8], (DEPTH, D_MODEL, N_IN), D_MODEL ** -0.5),
        'qk_gain': 1.0 + nrm(ks[9], (DEPTH, 6, HEAD_DIM), 0.1),
        'cmp_pos': nrm(ks[10], (DEPTH, 2, CMP_BLK, HEAD_DIM), 0.1),
        'cmp_w1': nrm(ks[11], (DEPTH, 2, CMP_BLK * HEAD_DIM, HEAD_DIM), (CMP_BLK * HEAD_DIM) ** -0.5),
        'cmp_w2': nrm(ks[12], (DEPTH, 2, HEAD_DIM, HEAD_DIM), HEAD_DIM ** -0.5),
        'conv_w': nrm(ks[13], (DEPTH, CONV_W, D_B), CONV_W ** -0.5),
        'conv_b': nrm(ks[14], (DEPTH, D_B), 0.02),
        'conv_ln_g': 1.0 + nrm(ks[15], (DEPTH, D_B), 0.1),
        'conv_ln_b': nrm(ks[16], (DEPTH, D_B), 0.02),
        'w_out': nrm(ks[17], (DEPTH, D_MODEL, D_MODEL), D_MODEL ** -0.5),
        'ln2_g': 1.0 + nrm(ks[18], (DEPTH, D_MODEL), 0.1),
        'w_up': nrm(ks[19], (DEPTH, D_MODEL, D_FF), D_MODEL ** -0.5),
        'w_down': nrm(ks[20], (DEPTH, D_FF, D_MODEL), D_FF ** -0.5),
    }


def reference(x_prompt, x_sample, cache_moba_kv, cache_nsa_kv, cache_win_kv, cache_conv, page_table,
              ln1_g, w_in, qk_gain, cmp_pos, cmp_w1, cmp_w2, conv_w, conv_b, conv_ln_g, conv_ln_b,
              w_out, ln2_g, w_up, w_down):
    bp, tp, _ = x_prompt.shape
    bs = x_sample.shape[0]
    past_len = page_table.shape[1] * cache_moba_kv.shape[2]
    dt = x_prompt.dtype
    yp, ys = x_prompt, x_sample
    kva_p, kva_s, kvc_p, kvc_s, win_p, win_s, conv_p, conv_s = [], [], [], [], [], [], [], []
    for l in range(DEPTH):
        lp = (ln1_g[l], w_in[l], qk_gain[l], cmp_pos[l], cmp_w1[l], cmp_w2[l], conv_w[l], conv_b[l],
              conv_ln_g[l], conv_ln_b[l], w_out[l], ln2_g[l], w_up[l], w_down[l])
        yp, a, c, w, cv = trunk_layer(
            yp, jnp.zeros((bp, 0, 2, H_A, HEAD_DIM), dt), jnp.zeros((bp, 0, 4, KVH_C, HEAD_DIM), dt),
            jnp.zeros((bp, 0, 2, KVH_C, HEAD_DIM), dt), jnp.zeros((bp, CONV_W - 1, D_B), dt),
            min(WINDOW, tp), *lp)
        kva_p.append(a); kvc_p.append(c); win_p.append(w); conv_p.append(cv)
        past_a = cache_moba_kv[l][page_table].reshape(bs, past_len, 2, H_A, HEAD_DIM)
        past_c = cache_nsa_kv[l][page_table].reshape(bs, past_len, 4, KVH_C, HEAD_DIM)
        ys, a, c, w, cv = trunk_layer(ys, past_a, past_c, cache_win_kv[l], cache_conv[l],
                                      cache_win_kv.shape[2], *lp)
        kva_s.append(a); kvc_s.append(c); win_s.append(w); conv_s.append(cv)
    return (yp, ys, jnp.stack(kva_p), jnp.stack(kva_s), jnp.stack(kvc_p), jnp.stack(kvc_s),
            jnp.stack(win_p), jnp.stack(win_s), jnp.stack(conv_p), jnp.stack(conv_s))
```

```python
import functools

import numpy as np
import jax
import jax.numpy as jnp
from jax import lax
from jax.experimental import pallas as pl
from jax.experimental.pallas import tpu as pltpu

F32 = jnp.float32
BF16 = jnp.bfloat16

HEAD_DIM = 64
SLOT = 2 * HEAD_DIM
PAGE_SIZE = 128
MOBA_BLOCK = 256
MOBA_TOPK = 3
CMP_STRIDE = 16
CMP_BLK = 32
SLC_BLK = 64
SLC_TOPK = 16
WINDOW = 512
CONV_W = 31
CONV_PAD = 32
EPS = 1e-6
FORCE_SCORE = 1e4
SCALE = HEAD_DIM ** -0.5
NEG_BIG = -(2.0 ** 100)
M_INIT = -(2.0 ** 99)
KV_TILE = 256
VMEM_LIMIT = 56 * 1024 * 1024

C_QA, C_KA, C_VA, C_GLU, C_QC, C_KVC, C_WIN, C_GATE, C_END = 0, 384, 768, 1152, 1664, 2432, 2944, 3200, 3328


def _alibi_slopes(n):
    return np.array([2.0 ** (-8.0 * (i + 1) / n) for i in range(n)], dtype=np.float32)


def _dot(a, b):
    return jnp.dot(a, b, preferred_element_type=F32)


def _dot_nt(a, b):
    return lax.dot_general(a, b, (((1,), (1,)), ((), ())), preferred_element_type=F32)


def _split_bf16(x):
    hi = x.astype(BF16)
    lo = (x - hi.astype(F32)).astype(BF16)
    return hi, lo


def _div_pow2(x, d):
    return lax.shift_right_logical(x, jnp.int32(int(d).bit_length() - 1))


def _sigmoid(x):
    return 1.0 / (1.0 + jnp.exp(-x))


def _params(sem, vmem=VMEM_LIMIT):
    return pltpu.CompilerParams(dimension_semantics=sem, vmem_limit_bytes=vmem)


def _head_rms(zs, seg, gain):
    cols = []
    for c in range(zs.shape[1] // SLOT):
        zz = zs[:, c * SLOT:(c + 1) * SLOT]
        hi, lo = _split_bf16(zz * zz)
        cols.append(_dot(hi, seg) + _dot(lo, seg))
    ms = cols[0] if len(cols) == 1 else jnp.concatenate(cols, axis=1)
    return zs * lax.rsqrt(ms + EPS) * gain


def _proj_body(x_ref, ln_ref, w_ref, seg_ref, gain_ref,
               qa_ref, kva_ref, kvab_ref, glu_ref, qcx_ref, kvc_ref, slcb_ref, win_ref, winb_ref,
               gsig_ref, kmean_ref):
    x = x_ref[...]
    ms = jnp.mean(x * x, axis=-1, keepdims=True)
    h = (x * lax.rsqrt(ms + EPS) * ln_ref[...]).astype(BF16)
    z = _dot(h, w_ref[...])
    seg = seg_ref[...]

    def normed(lo, hi):
        return _head_rms(z[:, lo:hi], seg, gain_ref[:, lo:hi])

    qa_ref[...] = normed(C_QA, C_KA).astype(BF16)
    ka = normed(C_KA, C_VA)
    kva = jnp.concatenate([ka, z[:, C_VA:C_GLU]], axis=1)
    kva_ref[...] = kva
    kvab_ref[...] = kva.astype(BF16)
    kmean_ref[0] = jnp.broadcast_to(jnp.sum(ka, axis=0, keepdims=True) * (1.0 / ka.shape[0]), (8, ka.shape[1]))
    half = (C_QC - C_GLU) // 2
    glu_ref[...] = z[:, C_GLU:C_GLU + half] * _sigmoid(z[:, C_GLU + half:C_QC])
    qcx_ref[...] = normed(C_QC, C_KVC).astype(BF16)
    slck = normed(C_KVC + 2 * SLOT, C_KVC + 3 * SLOT)
    kvc = jnp.concatenate([z[:, C_KVC:C_KVC + 2 * SLOT], slck, z[:, C_KVC + 3 * SLOT:C_WIN]], axis=1)
    kvc_ref[...] = kvc
    slcb_ref[...] = kvc[:, 2 * SLOT:].astype(BF16)
    wink = normed(C_WIN, C_WIN + SLOT)
    win = jnp.concatenate([wink, z[:, C_WIN + SLOT:C_GATE]], axis=1)
    win_ref[...] = win
    winb_ref[...] = win.astype(BF16)
    gsig_ref[...] = _sigmoid(z[:, C_GATE:C_END])


def _proj(x2d, ln_g, w_re, seg, gain_row, tm):
    m, d = x2d.shape
    nt = m // tm
    row = lambda w: pl.BlockSpec((tm, w), lambda i: (i, 0))
    full = lambda a: pl.BlockSpec(a.shape, lambda i: (0,) * a.ndim)
    outs = [(384, BF16), (768, F32), (768, BF16), (256, F32), (768, BF16), (512, F32), (256, BF16),
            (256, F32), (256, BF16), (128, F32)]
    return pl.pallas_call(
        _proj_body,
        grid=(nt,),
        in_specs=[row(d), full(ln_g), full(w_re), full(seg), full(gain_row)],
        out_specs=[row(w) for w, _ in outs] + [pl.BlockSpec((1, 8, 384), lambda i: (i, 0, 0))],
        out_shape=[jax.ShapeDtypeStruct((m, w), dt) for w, dt in outs]
        + [jax.ShapeDtypeStruct((nt, 8, 384), F32)],
        compiler_params=_params(("parallel",)),
        name="proj",
    )(x2d, ln_g, w_re, seg, gain_row)


def _topk_select(score, lane, k, sel):
    for _ in range(k):
        mx = jnp.max(score, axis=1, keepdims=True)
        idx = jnp.min(jnp.where(score == mx, lane, 1 << 20), axis=1, keepdims=True)
        hit = lane == idx
        sel = sel | (hit & (mx > -jnp.inf))
        score = jnp.where(hit, -jnp.inf, score)
    return sel


def _flash_step(carry, s, v):
    m, l, acc = carry
    m_new = jnp.maximum(m, jnp.max(s, axis=1, keepdims=True))
    alpha = jnp.exp(m - m_new)
    p = jnp.exp(s - m_new)
    l = alpha * l + jnp.sum(p, axis=1, keepdims=True)
    acc = alpha * acc + _dot(p.astype(BF16), v)
    return m_new, l, acc


def _masked_softmax_rows(s, mask):
    sm = jnp.where(mask, s, NEG_BIG)
    m = jnp.maximum(jnp.max(sm, axis=1, keepdims=True), M_INIT)
    e = jnp.where(mask, jnp.exp(sm - m), 0.0)
    l = jnp.sum(e, axis=1, keepdims=True)
    return e / jnp.where(l > 0.0, l, 1.0)


def _moba_body(sl_ref, q_ref, k_ref, v_ref, km_ref, o_ref, *, pos0, tq):
    pair = pl.program_id(1)
    t0 = pos0 + pl.program_id(2) * tq
    q = q_ref[0]
    lane = lax.broadcasted_iota(jnp.int32, (tq, SLOT), 1)
    t = t0 + lax.broadcasted_iota(jnp.int32, (tq, 1), 0)
    own = _div_pow2(t, MOBA_BLOCK)
    km_hi, km_lo = _split_bf16(km_ref[0])
    n_tiles = (t0 + tq - 1) // KV_TILE + 1
    col = lax.broadcasted_iota(jnp.int32, (tq, KV_TILE), 1)
    onehot_lane = lax.broadcasted_iota(jnp.int32, (KV_TILE, SLOT), 1)

    outs = []
    for hh in range(2):
        qh = jnp.where((lane >= HEAD_DIM) == bool(hh), q, jnp.zeros_like(q))
        gate = _dot_nt(qh, km_hi) + _dot_nt(qh, km_lo)
        gate = jnp.where(lane < own, gate, -jnp.inf)
        sel = _topk_select(gate, lane, MOBA_TOPK, lane == own)
        qaug = jnp.concatenate([qh, jnp.where(sel, 0.0, NEG_BIG).astype(BF16)], axis=1)
        slope = sl_ref[2 * pair + hh]

        def step(n, carry, qaug=qaug, slope=slope):
            off = pl.multiple_of(n * KV_TILE, KV_TILE)
            kaug = jnp.concatenate([k_ref[0, pl.ds(off, KV_TILE), :], (onehot_lane == n).astype(BF16)], axis=1)
            dist = t - (n * KV_TILE + col)
            s = _dot_nt(qaug, kaug) - slope * dist.astype(F32)
            s = jnp.where(dist >= 0, s, NEG_BIG)
            return _flash_step(carry, s, v_ref[0, pl.ds(off, KV_TILE), :])

        init = (jnp.full((tq, 1), M_INIT, F32), jnp.zeros((tq, 1), F32), jnp.zeros((tq, SLOT), F32))
        _, l, acc = lax.fori_loop(0, n_tiles, step, init)
        outs.append(acc / l)
    o_ref[0] = jnp.where(lane < HEAD_DIM, outs[0], outs[1]).astype(BF16)


def _moba(q, kvb, kmean, slopes, pos0, tq):
    b, nq, wa = q.shape
    l_pad = kvb.shape[1]
    npair = wa // SLOT
    grid_spec = pltpu.PrefetchScalarGridSpec(
        num_scalar_prefetch=1,
        grid=(b, npair, nq // tq),
        in_specs=[
            pl.BlockSpec((1, tq, SLOT), lambda bi, p, i, sl: (bi, i, p)),
            pl.BlockSpec((1, l_pad, SLOT), lambda bi, p, i, sl: (bi, 0, p)),
            pl.BlockSpec((1, l_pad, SLOT), lambda bi, p, i, sl: (bi, 0, npair + p)),
            pl.BlockSpec((1, SLOT, SLOT), lambda bi, p, i, sl: (bi, 0, p)),
        ],
        out_specs=pl.BlockSpec((1, tq, SLOT), lambda bi, p, i, sl: (bi, i, p)),
    )
    return pl.pallas_call(
        functools.partial(_moba_body, pos0=pos0, tq=tq),
        grid_spec=grid_spec,
        out_shape=jax.ShapeDtypeStruct((b, nq, wa), BF16),
        compiler_params=_params(("parallel", "parallel", "arbitrary")),
        name="moba",
    )(slopes, q, kvb, kvb, kmean)


def _cmp1_body(rawk_ref, rawv_ref, pos_ref, w_ref, u_ref, *, rows):
    nc = rows // CMP_STRIDE
    acc = [jnp.zeros((nc, SLOT), F32) for _ in range(4)]
    for r in range(CMP_STRIDE):
        for kv, raw_ref in enumerate((rawk_ref, rawv_ref)):
            xs = raw_ref[0, pl.ds(r, nc, stride=CMP_STRIDE), :]
            for half in range(2):
                rr = half * CMP_STRIDE + r
                xp = (xs + pos_ref[rr:rr + 1, kv * SLOT:(kv + 1) * SLOT]).astype(BF16)
                acc[2 * kv + half] = acc[2 * kv + half] + _dot(xp, w_ref[kv, half, r])
    u_ref[0] = jnp.concatenate(acc, axis=1)


def _cmp1(raw, pos_e, w1e, rows):
    b, l_pad, _ = raw.shape
    nc = rows // CMP_STRIDE
    return pl.pallas_call(
        functools.partial(_cmp1_body, rows=rows),
        grid=(b, l_pad // rows),
        in_specs=[
            pl.BlockSpec((1, rows, SLOT), lambda bi, i: (bi, i, 0)),
            pl.BlockSpec((1, rows, SLOT), lambda bi, i: (bi, i, 1)),
            pl.BlockSpec(pos_e.shape, lambda bi, i: (0, 0)),
            pl.BlockSpec(w1e.shape, lambda bi, i: (0, 0, 0, 0, 0)),
        ],
        out_specs=pl.BlockSpec((1, nc, 4 * SLOT), lambda bi, i: (bi, i, 0)),
        out_shape=jax.ShapeDtypeStruct((b, l_pad // CMP_STRIDE, 4 * SLOT), F32),
        compiler_params=_params(("parallel", "parallel")),
        name="cmp1",
    )(raw, raw, pos_e, w1e)


def _cmp2_body(ua_ref, ub_ref, w2_ref, seg_ref, gain_ref, kc_ref, vc_ref):
    ua = ua_ref[0]
    ub = ub_ref[0]
    pre_k = ua[:, 0:SLOT] + ub[:, SLOT:2 * SLOT]
    pre_v = ua[:, 2 * SLOT:3 * SLOT] + ub[:, 3 * SLOT:4 * SLOT]
    ck = _dot(jax.nn.gelu(pre_k).astype(BF16), w2_ref[0])
    cv = _dot(jax.nn.gelu(pre_v).astype(BF16), w2_ref[1])
    kc_ref[0] = _head_rms(ck, seg_ref[...], gain_ref[...]).astype(BF16)
    vc_ref[0] = cv.astype(BF16)


def _cmp2(ua, ub, w2e, seg, gain3):
    b, ncp, _ = ua.shape
    blk = pl.BlockSpec((1, ncp, 4 * SLOT), lambda bi: (bi, 0, 0))
    out = pl.BlockSpec((1, ncp, SLOT), lambda bi: (bi, 0, 0))
    return pl.pallas_call(
        _cmp2_body,
        grid=(b,),
        in_specs=[blk, blk, pl.BlockSpec(w2e.shape, lambda bi: (0, 0, 0)),
                  pl.BlockSpec(seg.shape, lambda bi: (0, 0)), pl.BlockSpec(gain3.shape, lambda bi: (0, 0))],
        out_specs=[out, out],
        out_shape=[jax.ShapeDtypeStruct((b, ncp, SLOT), BF16)] * 2,
        compiler_params=_params(("parallel",)),
        name="cmp2",
    )(ua, ub, w2e, seg, gain3)


def _nsa_sel_body(sl_ref, q_ref, g_ref, kc_ref, vc_ref, ov_ref, wk_ref, wv_ref, part_ref, g1_ref, selb_ref, *,
                  pos0, tq, n_cmp, pw0, lw_valid):
    t0 = pos0 + pl.program_id(1) * tq
    q = q_ref[0]
    gs = g_ref[0]
    ncp = kc_ref.shape[1]
    nsp = ov_ref.shape[1]
    n_wtiles = wk_ref.shape[1] // KV_TILE
    t = t0 + lax.broadcasted_iota(jnp.int32, (tq, 1), 0)
    t3 = jnp.concatenate([t, t, t], axis=0)
    lane = lax.broadcasted_iota(jnp.int32, (tq, SLOT), 1)
    mcol = lax.broadcasted_iota(jnp.int32, (tq, nsp), 1)
    col = lax.broadcasted_iota(jnp.int32, (1, KV_TILE), 1)
    own = _div_pow2(t, SLC_BLK)

    parts, g1s, selbs = [], [], []
    for g in range(2):
        q3 = jnp.concatenate([q[:, (3 * g + r) * SLOT:(3 * g + r + 1) * SLOT] for r in range(3)], axis=0)
        slope3 = jnp.concatenate([jnp.full((tq, 1), sl_ref[3 * g + r], F32) for r in range(3)], axis=0)

        cend = lax.broadcasted_iota(jnp.int32, (1, ncp), 1) * CMP_STRIDE + (CMP_BLK - 1)
        dist = t3 - cend
        s = _dot_nt(q3, kc_ref[0]) - slope3 * dist.astype(F32)
        mask = (dist >= 0) & (cend < n_cmp * CMP_STRIDE + (CMP_BLK - 1))
        p = _masked_softmax_rows(s, mask)
        o_cmp = _dot(p.astype(BF16), vc_ref[0])
        p_hi, p_lo = _split_bf16(p[0:tq] + p[tq:2 * tq] + p[2 * tq:3 * tq])
        imp = _dot(p_hi, ov_ref[...]) + _dot(p_lo, ov_ref[...])

        forced = (mcol == 0) | (mcol == own) | (mcol == own - 1)
        score = jnp.where(forced, FORCE_SCORE, jnp.where(mcol <= own, imp, -jnp.inf))
        sel = _topk_select(score, mcol, SLC_TOPK, jnp.zeros((tq, nsp), jnp.bool_))
        selbs.append(jnp.where(sel, 0.0, NEG_BIG).astype(BF16))

        wb = (t0 - pw0) // KV_TILE - 2
        s_parts, v_parts, m_parts = [], [], []
        for j in range(3):
            tile = wb + j
            off = pl.multiple_of(jnp.clip(tile, 0, n_wtiles - 1) * KV_TILE, KV_TILE)
            ridx = tile * KV_TILE + col
            dist = t3 - (pw0 + ridx)
            s_parts.append(_dot_nt(q3, wk_ref[0, pl.ds(off, KV_TILE), :]) - slope3 * dist.astype(F32))
            m_parts.append((ridx >= 0) & (ridx < lw_valid) & (dist >= 0) & (dist <= WINDOW))
            v_parts.append(wv_ref[0, pl.ds(off, KV_TILE), :])
        p = _masked_softmax_rows(jnp.concatenate(s_parts, axis=1), jnp.concatenate(m_parts, axis=1))
        o_win = sum(_dot(p[:, j * KV_TILE:(j + 1) * KV_TILE].astype(BF16), v_parts[j]) for j in range(3))

        in_group = (lane >= HEAD_DIM) if g else (lane < HEAD_DIM)
        for r in range(3):
            hd = 3 * g + r
            rows = slice(r * tq, (r + 1) * tq)
            o = gs[:, 3 * hd:3 * hd + 1] * o_cmp[rows] + gs[:, 3 * hd + 2:3 * hd + 3] * o_win[rows]
            parts.append(jnp.where(in_group, o, 0.0))
            g1s.append(jnp.where(in_group, gs[:, 3 * hd + 1:3 * hd + 2], 0.0))
    part_ref[0] = jnp.concatenate(parts, axis=1)
    g1_ref[0] = jnp.concatenate(g1s, axis=1)
    selb_ref[0] = jnp.concatenate(selbs, axis=1)


def _nsa_sel(qcx, gsig, kc, vc, ov, winb, slopes, pos0, tq, n_cmp, pw0, lw_valid):
    b, nq, wq = qcx.shape
    lw = winb.shape[1]
    ncp = kc.shape[1]
    nsp = ov.shape[1]
    grid_spec = pltpu.PrefetchScalarGridSpec(
        num_scalar_prefetch=1,
        grid=(b, nq // tq),
        in_specs=[
            pl.BlockSpec((1, tq, wq), lambda bi, i, sl: (bi, i, 0)),
            pl.BlockSpec((1, tq, SLOT), lambda bi, i, sl: (bi, i, 0)),
            pl.BlockSpec((1, ncp, SLOT), lambda bi, i, sl: (bi, 0, 0)),
            pl.BlockSpec((1, ncp, SLOT), lambda bi, i, sl: (bi, 0, 0)),
            pl.BlockSpec(ov.shape, lambda bi, i, sl: (0, 0)),
            pl.BlockSpec((1, lw, SLOT), lambda bi, i, sl: (bi, 0, 0)),
            pl.BlockSpec((1, lw, SLOT), lambda bi, i, sl: (bi, 0, 1)),
        ],
        out_specs=[pl.BlockSpec((1, tq, wq), lambda bi, i, sl: (bi, i, 0)),
                   pl.BlockSpec((1, tq, wq), lambda bi, i, sl: (bi, i, 0)),
                   pl.BlockSpec((1, tq, 2 * nsp), lambda bi, i, sl: (bi, i, 0))],
    )
    return pl.pallas_call(
        functools.partial(_nsa_sel_body, pos0=pos0, tq=tq, n_cmp=n_cmp, pw0=pw0, lw_valid=lw_valid),
        grid_spec=grid_spec,
        out_shape=[jax.ShapeDtypeStruct((b, nq, wq), F32), jax.ShapeDtypeStruct((b, nq, wq), F32),
                   jax.ShapeDtypeStruct((b, nq, 2 * nsp), BF16)],
        compiler_params=_params(("parallel", "arbitrary")),
        name="nsa_sel",
    )(slopes, qcx, gsig, kc, vc, ov, winb, winb)


def _nsa_slc_body(sl_ref, q_ref, selb_ref, k_ref, v_ref, part_ref, g1_ref, o_ref, *, pos0, tq):
    hd = pl.program_id(1)
    t0 = pos0 + pl.program_id(2) * tq
    nsp = selb_ref.shape[2]
    t = t0 + lax.broadcasted_iota(jnp.int32, (tq, 1), 0)
    n_tiles = (t0 + tq - 1) // KV_TILE + 1
    col = lax.broadcasted_iota(jnp.int32, (tq, KV_TILE), 1)
    blk_lane = lax.broadcasted_iota(jnp.int32, (KV_TILE, nsp), 1)
    blk_row = _div_pow2(lax.broadcasted_iota(jnp.int32, (KV_TILE, nsp), 0), SLC_BLK)
    qaug = jnp.concatenate([q_ref[0], selb_ref[0]], axis=1)
    slope = sl_ref[hd]

    def step(n, carry):
        off = pl.multiple_of(n * KV_TILE, KV_TILE)
        onehot = (blk_lane == n * (KV_TILE // SLC_BLK) + blk_row).astype(BF16)
        kaug = jnp.concatenate([k_ref[0, pl.ds(off, KV_TILE), :], onehot], axis=1)
        dist = t - (n * KV_TILE + col)
        s = _dot_nt(qaug, kaug) - slope * dist.astype(F32)
        s = jnp.where(dist >= 0, s, NEG_BIG)
        return _flash_step(carry, s, v_ref[0, pl.ds(off, KV_TILE), :])

    init = (jnp.full((tq, 1), M_INIT, F32), jnp.zeros((tq, 1), F32), jnp.zeros((tq, SLOT), F32))
    _, l, acc = lax.fori_loop(0, n_tiles, step, init)
    o_ref[0] = (part_ref[0] + g1_ref[0] * (acc / l)).astype(BF16)


def _nsa_slc(qcx, selb, slcb, part, g1x, slopes, pos0, tq):
    b, nq, wq = qcx.shape
    l_pad = slcb.shape[1]
    nsp = selb.shape[2] // 2
    nh = wq // SLOT
    slot = pl.BlockSpec((1, tq, SLOT), lambda bi, h, i, sl: (bi, i, h))
    grid_spec = pltpu.PrefetchScalarGridSpec(
        num_scalar_prefetch=1,
        grid=(b, nh, nq // tq),
        in_specs=[
            slot,
            pl.BlockSpec((1, tq, nsp), lambda bi, h, i, sl: (bi, i, h // (nh // 2))),
            pl.BlockSpec((1, l_pad, SLOT), lambda bi, h, i, sl: (bi, 0, 0)),
            pl.BlockSpec((1, l_pad, SLOT), lambda bi, h, i, sl: (bi, 0, 1)),
            slot, slot,
        ],
        out_specs=slot,
    )
    return pl.pallas_call(
        functools.partial(_nsa_slc_body, pos0=pos0, tq=tq),
        grid_spec=grid_spec,
        out_shape=jax.ShapeDtypeStruct((b, nq, wq), BF16),
        compiler_params=_params(("parallel", "parallel", "arbitrary")),
        name="nsa_slc",
    )(slopes, qcx, selb, slcb, slcb, part, g1x)


def _conv_body(cin_ref, cw_ref, cb_ref, lg_ref, lb_ref, y_ref, *, tm):
    off = pl.multiple_of(pl.program_id(1) * tm, 8)
    win = cin_ref[0, pl.ds(off, tm + CONV_PAD), :]
    acc = jnp.zeros((tm, win.shape[1]), F32)
    for w in range(CONV_W):
        lo = w + CONV_PAD - (CONV_W - 1)
        acc = acc + cw_ref[w:w + 1, :] * win[lo:lo + tm, :]
    acc = acc + cb_ref[...]
    mu = jnp.mean(acc, axis=-1, keepdims=True)
    cen = acc - mu
    var = jnp.mean(cen * cen, axis=-1, keepdims=True)
    yn = cen * lax.rsqrt(var + EPS) * lg_ref[...] + lb_ref[...]
    y_ref[0] = (yn * _sigmoid(yn)).astype(BF16)


def _conv(cin, cw, cb, lg, lb, tm):
    b, rows, ch = cin.shape
    t_pad = rows - CONV_PAD
    full2 = lambda a: pl.BlockSpec(a.shape, lambda bi, i: (0, 0))
    return pl.pallas_call(
        functools.partial(_conv_body, tm=tm),
        grid=(b, t_pad // tm),
        in_specs=[pl.BlockSpec((1, rows, ch), lambda bi, i: (bi, 0, 0)), full2(cw), full2(cb), full2(lg), full2(lb)],
        out_specs=pl.BlockSpec((1, tm, ch), lambda bi, i: (bi, i, 0)),
        out_shape=jax.ShapeDtypeStruct((b, t_pad, ch), BF16),
        compiler_params=_params(("parallel", "arbitrary")),
        name="conv",
    )(cin, cw, cb, lg, lb)


def _ffn_body(x_ref, oa_ref, y_ref, oc_ref, wa_ref, wy_ref, wc_ref, g2_ref, wu_ref, wd_ref, out_ref):
    x1 = x_ref[...] + _dot(oa_ref[...], wa_ref[...]) + _dot(y_ref[...], wy_ref[...]) + _dot(oc_ref[...], wc_ref[...])
    ms = jnp.mean(x1 * x1, axis=-1, keepdims=True)
    h2 = (x1 * lax.rsqrt(ms + EPS) * g2_ref[...]).astype(BF16)
    u = jnp.maximum(_dot(h2, wu_ref[...]), 0.0)
    out_ref[...] = x1 + _dot((u * u).astype(BF16), wd_ref[...])


def _ffn(x2d, oa, y, oc, wa, wy, wc, g2, wu, wd, tm):
    m, d = x2d.shape
    row = lambda a: pl.BlockSpec((tm, a.shape[1]), lambda i: (i, 0))
    const = lambda a: pl.BlockSpec(a.shape, lambda i: (0, 0), pipeline_mode=pl.Buffered(1))
    return pl.pallas_call(
        _ffn_body,
        grid=(m // tm,),
        in_specs=[row(x2d), row(oa), row(y), row(oc), const(wa), const(wy), const(wc), const(g2), const(wu), const(wd)],
        out_specs=pl.BlockSpec((tm, d), lambda i: (i, 0)),
        out_shape=jax.ShapeDtypeStruct((m, d), F32),
        compiler_params=_params(("parallel",)),
        name="out_ffn",
    )(x2d, oa, y, oc, wa, wy, wc, g2, wu, wd)


def _gather_moba_body(pt_ref, p0_ref, p1_ref, tail_ref, kvb_ref, km_ref, *, n_past):
    blk = jnp.concatenate([p0_ref[0], p1_ref[0]], axis=0)
    blk = jnp.where(pl.program_id(1) < n_past, blk, tail_ref[0])
    kvb_ref[0] = blk.astype(BF16)
    wk = km_ref.shape[-1]
    km_ref[0, 0] = jnp.sum(blk[:, :wk], axis=0, keepdims=True) * (1.0 / MOBA_BLOCK)


def _gather_nsa_body(pt_ref, p0_ref, p1_ref, tail_ref, raw_ref, slcb_ref, *, n_past):
    blk = jnp.concatenate([p0_ref[0], p1_ref[0]], axis=0)
    blk = jnp.where(pl.program_id(1) < n_past, blk, tail_ref[0])
    raw_ref[0] = blk[:, :2 * SLOT]
    slcb_ref[0] = blk[:, 2 * SLOT:].astype(BF16)


def _gather(body, name, cache, tail, page_table, out_widths, out_dtypes, km_width=None):
    bsz, n_pages = page_table.shape
    w = cache.shape[-1]
    n_past = n_pages * PAGE_SIZE // KV_TILE
    nb = n_past + 1

    def page(k):
        return pl.BlockSpec((1, PAGE_SIZE, w),
                            lambda bi, n, pt: (pt[bi, jnp.minimum(2 * n + k, n_pages - 1)], 0, 0))

    out_specs = [pl.BlockSpec((1, KV_TILE, ow), lambda bi, n, pt: (bi, n, 0)) for ow in out_widths]
    out_shape = [jax.ShapeDtypeStruct((bsz, nb * KV_TILE, ow), dt) for ow, dt in zip(out_widths, out_dtypes)]
    if km_width is not None:
        out_specs.append(pl.BlockSpec((1, 1, 1, km_width), lambda bi, n, pt: (bi, n, 0, 0)))
        out_shape.append(jax.ShapeDtypeStruct((bsz, nb, 1, km_width), F32))
    grid_spec = pltpu.PrefetchScalarGridSpec(
        num_scalar_prefetch=1,
        grid=(bsz, nb),
        in_specs=[page(0), page(1), pl.BlockSpec((1, KV_TILE, w), lambda bi, n, pt: (bi, 0, 0))],
        out_specs=out_specs,
    )
    return pl.pallas_call(
        functools.partial(body, n_past=n_past),
        grid_spec=grid_spec,
        out_shape=out_shape,
        compiler_params=_params(("parallel", "arbitrary")),
        name=name,
    )(page_table, cache, cache, tail)


def _layer_consts(lp):
    (ln1_g, w_in, qk_gain, cmp_pos, cmp_w1, cmp_w2, conv_w, conv_b, conv_ln_g, conv_ln_b, w_out, ln2_g, w_up,
     w_down) = lp
    d = w_in.shape[0]
    z64 = jnp.zeros((d, HEAD_DIM), F32)
    qc = w_in[:, 1664:2048].reshape(d, 6, HEAD_DIM)
    qc_slots = [jnp.concatenate([qc[:, h], z64] if h < 3 else [z64, qc[:, h]], axis=1) for h in range(6)]
    gate = jnp.pad(w_in[:, 2816:2834], ((0, 0), (0, SLOT - 18)))
    w_re = jnp.concatenate([w_in[:, 0:1664]] + qc_slots + [w_in[:, 2048:2816], gate], axis=1).astype(BF16)

    ones = lambda n: jnp.ones((n,), F32)
    g = qk_gain
    gain_row = jnp.concatenate([
        jnp.tile(g[0], 6) * SCALE, jnp.tile(g[1], 6), ones(C_QC - C_VA),
        jnp.tile(g[2], 12) * SCALE, ones(2 * SLOT), jnp.tile(g[4], 2), ones(SLOT),
        jnp.tile(g[5], 2), ones(SLOT), ones(SLOT)])[None, :]
    seg = jnp.asarray(np.kron(np.eye(2, dtype=np.float32), np.full((HEAD_DIM, HEAD_DIM), 1.0 / HEAD_DIM, np.float32)),
                      BF16)

    eye2 = jnp.eye(2, dtype=F32)
    w1 = cmp_w1.reshape(2, 2, CMP_STRIDE, HEAD_DIM, HEAD_DIM)
    w1e = jnp.einsum("ab,khrdj->khradbj", eye2, w1).reshape(2, 2, CMP_STRIDE, SLOT, SLOT).astype(BF16)
    w2e = jnp.einsum("ab,kdj->kadbj", eye2, cmp_w2).reshape(2, SLOT, SLOT).astype(BF16)
    pos_e = jnp.concatenate([jnp.tile(cmp_pos[0], (1, 2)), jnp.tile(cmp_pos[1], (1, 2))], axis=1)
    gain3 = jnp.tile(g[3], 2)[None, :]

    wo = w_out
    wa = wo[0:384].astype(BF16)
    wy = wo[384:640].astype(BF16)
    wc_rows = wo[640:1024].reshape(6, HEAD_DIM, d)
    zr = jnp.zeros((HEAD_DIM, d), F32)
    wc = jnp.concatenate([jnp.concatenate([wc_rows[h], zr] if h < 3 else [zr, wc_rows[h]], axis=0)
                          for h in range(6)], axis=0).astype(BF16)
    return dict(ln1=ln1_g[None, :], w_re=w_re, gain_row=gain_row, seg=seg, w1e=w1e, w2e=w2e, pos_e=pos_e,
                gain3=gain3, conv_w=conv_w, conv_b=conv_b[None, :], conv_lg=conv_ln_g[None, :],
                conv_lb=conv_ln_b[None, :], wa=wa, wy=wy, wc=wc, ln2=ln2_g[None, :],
                wu=w_up.astype(BF16), wd=w_down.astype(BF16))


def _overlap_matrix(ncp, nsp, n_cmp, n_slc):
    n = np.arange(ncp)[:, None]
    m = np.arange(nsp)[None, :]
    ov = ((m == (n * CMP_STRIDE) // SLC_BLK) | (m == (n * CMP_STRIDE + CMP_BLK - 1) // SLC_BLK))
    ov = ov & (n < n_cmp) & (m < n_slc)
    return jnp.asarray(ov.astype(np.float32), BF16)


def _round_up(x, m):
    return -(-x // m) * m


def _pick_rows(l_pad):
    for rows in (2048, 1280, 1024, 768, 512, 256):
        if l_pad % rows == 0:
            return rows
    raise ValueError(l_pad)


def _layer(x, c, past, slopes_a, slopes_c):
    b, t, d = x.shape
    m = b * t
    tm = 256 if m % 256 == 0 else m
    qa, kva, kvab, glu, qcx, kvc, slcb, win, winb, gsig, kmean_t = _proj(
        x.reshape(m, d), c["ln1"], c["w_re"], c["seg"], c["gain_row"], tm)
    r3 = lambda a: a.reshape(b, t, a.shape[-1])
    qa, kva, kvab, glu, qcx, kvc, slcb, win, winb, gsig = map(r3, (qa, kva, kvab, glu, qcx, kvc, slcb, win, winb, gsig))

    if past is None:
        past_len = 0
        tq = KV_TILE
        assert t % KV_TILE == 0
        l_pad = t
        nb = t // MOBA_BLOCK
        kmean = kmean_t[:, 0, :].reshape(b, nb, 384)
        raw = kvc
        raw_spec_rows = _pick_rows(l_pad)
        win_all = winb
        lw_valid = t
        pw0 = 0
        q_a, q_c, g_c = qa, qcx, gsig
        new_win_state = win[:, -min(WINDOW, t):]
        conv_in = jnp.pad(glu, ((0, 0), (CONV_PAD, 0), (0, 0)))
        new_conv = glu[:, -(CONV_W - 1):]
        conv_tm = 256
    else:
        page_table = past["page_table"]
        past_len = page_table.shape[1] * PAGE_SIZE
        assert past_len % KV_TILE == 0 and t <= 16
        tq = 16
        l_pad = past_len + KV_TILE
        nb = l_pad // MOBA_BLOCK
        padt = lambda a: jnp.pad(a, ((0, 0), (0, KV_TILE - t), (0, 0)))
        kvab, km4 = _gather(_gather_moba_body, "gather_moba", past["moba"], padt(kva), page_table,
                            [768], [BF16], km_width=384)
        kmean = km4[:, :, 0, :]
        raw, slcb = _gather(_gather_nsa_body, "gather_nsa", past["nsa"], padt(kvc), page_table,
                            [2 * SLOT, 2 * SLOT], [F32, BF16])
        raw_spec_rows = _pick_rows(l_pad)
        win_buf = past["win"]
        nbuf = win_buf.shape[1]
        lw_valid = nbuf + t
        win_cat = jnp.concatenate([win_buf, win], axis=1)
        win_all = jnp.pad(win_cat, ((0, 0), (0, _round_up(lw_valid, KV_TILE) - lw_valid), (0, 0))).astype(BF16)
        pw0 = past_len - nbuf
        padq = lambda a: jnp.pad(a, ((0, 0), (0, tq - t), (0, 0)))
        q_a, q_c, g_c = padq(qa), padq(qcx), padq(gsig)
        new_win_state = win_cat[:, -nbuf:]
        conv_cat = jnp.concatenate([past["conv"], glu], axis=1)
        conv_in = jnp.pad(conv_cat, ((0, 0), (CONV_PAD - (CONV_W - 1), 8 - t), (0, 0)))
        new_conv = conv_cat[:, -(CONV_W - 1):]
        conv_tm = 8

    kmean = jnp.pad(kmean, ((0, 0), (0, SLOT - nb), (0, 0)))
    o_a = _moba(q_a, kvab, kmean, slopes_a, past_len, tq)[:, :t]

    n_chunks = l_pad // CMP_STRIDE
    n_cmp = n_chunks - 1
    n_slc = l_pad // SLC_BLK
    ncp = _round_up(n_cmp, SLOT)
    nsp = _round_up(n_slc, SLOT)
    u = _cmp1(raw, c["pos_e"], c["w1e"], raw_spec_rows)
    ua = jnp.pad(u[:, :n_cmp], ((0, 0), (0, ncp - n_cmp), (0, 0)))
    ub = jnp.pad(u[:, 1:], ((0, 0), (0, ncp - n_cmp), (0, 0)))
    kc, vc = _cmp2(ua, ub, c["w2e"], c["seg"], c["gain3"])
    ov = _overlap_matrix(ncp, nsp, n_cmp, n_slc)
    part, g1x, selb = _nsa_sel(q_c, g_c, kc, vc, ov, win_all, slopes_c, past_len, tq, n_cmp, pw0, lw_valid)
    o_c = _nsa_slc(q_c, selb, slcb, part, g1x, slopes_c, past_len, tq)[:, :t]

    y = _conv(conv_in, c["conv_w"], c["conv_b"], c["conv_lg"], c["conv_lb"], conv_tm)[:, :t]

    f2 = lambda a: a.reshape(m, a.shape[-1])
    x_new = _ffn(x.reshape(m, d), f2(o_a), f2(y), f2(o_c), c["wa"], c["wy"], c["wc"], c["ln2"], c["wu"], c["wd"], tm)
    return x_new.reshape(b, t, d), kva, kvc, new_win_state, new_conv


def kernel(x_prompt, x_sample, cache_moba_kv, cache_nsa_kv, cache_win_kv, cache_conv, page_table, ln1_g, w_in,
           qk_gain, cmp_pos, cmp_w1, cmp_w2, conv_w, conv_b, conv_ln_g, conv_ln_b, w_out, ln2_g, w_up, w_down):
    depth = w_in.shape[0]
    bp, tp, _ = x_prompt.shape
    bs, ts, _ = x_sample.shape
    n_pool = cache_moba_kv.shape[1]
    slopes_a = jnp.asarray(_alibi_slopes(6))
    slopes_c = jnp.asarray(_alibi_slopes(6))
    yp, ys = x_prompt, x_sample
    outs = [[] for _ in range(8)]
    for l in range(depth):
        c = _layer_consts((ln1_g[l], w_in[l], qk_gain[l], cmp_pos[l], cmp_w1[l], cmp_w2[l], conv_w[l], conv_b[l],
                           conv_ln_g[l], conv_ln_b[l], w_out[l], ln2_g[l], w_up[l], w_down[l]))
        yp, a, cc, w, cv = _layer(yp, c, None, slopes_a, slopes_c)
        outs[0].append(a.reshape(bp, tp, 2, 6, HEAD_DIM))
        outs[2].append(cc.reshape(bp, tp, 4, 2, HEAD_DIM))
        outs[4].append(w.reshape(bp, w.shape[1], 2, 2, HEAD_DIM))
        outs[6].append(cv)
        past = dict(page_table=page_table + l * n_pool,
                    moba=cache_moba_kv.reshape(depth * n_pool, PAGE_SIZE, 768),
                    nsa=cache_nsa_kv.reshape(depth * n_pool, PAGE_SIZE, 512),
                    win=cache_win_kv[l].reshape(bs, cache_win_kv.shape[2], 256),
                    conv=cache_conv[l])
        ys, a, cc, w, cv = _layer(ys, c, past, slopes_a, slopes_c)
        outs[1].append(a.reshape(bs, ts, 2, 6, HEAD_DIM))
        outs[3].append(cc.reshape(bs, ts, 4, 2, HEAD_DIM))
        outs[5].append(w.reshape(bs, w.shape[1], 2, 2, HEAD_DIM))
        outs[7].append(cv)
    return (yp, ys) + tuple(jnp.stack(o) for o in outs)
```

```python
import functools

import numpy as np
import jax
import jax.numpy as jnp
from jax import lax
from jax.experimental import pallas as pl
from jax.experimental.pallas import tpu as pltpu

F32 = jnp.float32
BF16 = jnp.bfloat16

HEAD_DIM = 64
SLOT = 2 * HEAD_DIM
N_HEADS = 6
PAGE_SIZE = 128
MOBA_BLOCK = 256
MOBA_TOPK = 3
CMP_STRIDE = 16
CMP_BLK = 32
SLC_BLK = 64
SLC_TOPK = 16
WINDOW = 512
CONV_W = 31
CONV_PAD = 32
EPS = 1e-6
FORCE_SCORE = 1e4
SCALE = HEAD_DIM ** -0.5
NEG_BIG = -(2.0 ** 100)
M_INIT = -(2.0 ** 99)
KV_TILE = 256
DEC_ROWS = 16
N_COEF = 3
VMEM_LIMIT = 56 * 1024 * 1024

C_QA, C_KA, C_VA, C_GLU, C_QC, C_KVC, C_WIN, C_GATE, C_END = 0, 384, 768, 1152, 1664, 2432, 2944, 3200, 3328


def _alibi_slopes(n):
    return np.array([2.0 ** (-8.0 * (i + 1) / n) for i in range(n)], dtype=np.float32)


def _slope_table(n):
    sl = _alibi_slopes(n)
    terms, rest = [], sl.copy()
    for _ in range(N_COEF):
        term = rest.astype(BF16).astype(np.float32)
        terms.append(term)
        rest = rest - term
    assert not rest.any()
    return np.stack(terms + [sl], axis=1)


def _dot(a, b):
    return jnp.dot(a, b, preferred_element_type=F32)


def _dot_nt(a, b):
    return lax.dot_general(a, b, (((1,), (1,)), ((), ())), preferred_element_type=F32)


def _split_bf16(x):
    hi = x.astype(BF16)
    lo = (x - hi.astype(F32)).astype(BF16)
    return hi, lo


def _div_pow2(x, d):
    return lax.shift_right_logical(x, jnp.int32(int(d).bit_length() - 1))


def _sigmoid(x):
    return 1.0 / (1.0 + jnp.exp(-x))


def _params(sem, vmem=VMEM_LIMIT):
    return pltpu.CompilerParams(dimension_semantics=sem, vmem_limit_bytes=vmem)


def _head_rms(zs, seg, gain):
    cols = []
    for c in range(zs.shape[1] // SLOT):
        zz = zs[:, c * SLOT:(c + 1) * SLOT]
        hi, lo = _split_bf16(zz * zz)
        cols.append(_dot(hi, seg) + _dot(lo, seg))
    ms = cols[0] if len(cols) == 1 else jnp.concatenate(cols, axis=1)
    return zs * lax.rsqrt(ms + EPS) * gain


def _proj_body(x_ref, ln_ref, w_ref, seg_ref, gain_ref,
               qa_ref, kva_ref, kvab_ref, glu_ref, qcx_ref, kvc_ref, slcb_ref, win_ref, winb_ref,
               gsig_ref, kmean_ref):
    x = x_ref[...]
    ms = jnp.mean(x * x, axis=-1, keepdims=True)
    h = (x * lax.rsqrt(ms + EPS) * ln_ref[...]).astype(BF16)
    z = _dot(h, w_ref[...])
    seg = seg_ref[...]

    def normed(lo, hi):
        return _head_rms(z[:, lo:hi], seg, gain_ref[:, lo:hi])

    qa_ref[...] = normed(C_QA, C_KA).astype(BF16)
    ka = normed(C_KA, C_VA)
    kva = jnp.concatenate([ka, z[:, C_VA:C_GLU]], axis=1)
    kva_ref[...] = kva
    kvab_ref[...] = kva.astype(BF16)
    kmean_ref[0] = jnp.broadcast_to(jnp.sum(ka, axis=0, keepdims=True) * (1.0 / ka.shape[0]), (8, ka.shape[1]))
    half = (C_QC - C_GLU) // 2
    glu_ref[...] = z[:, C_GLU:C_GLU + half] * _sigmoid(z[:, C_GLU + half:C_QC])
    qcx_ref[...] = normed(C_QC, C_KVC).astype(BF16)
    slck = normed(C_KVC + 2 * SLOT, C_KVC + 3 * SLOT)
    kvc = jnp.concatenate([z[:, C_KVC:C_KVC + 2 * SLOT], slck, z[:, C_KVC + 3 * SLOT:C_WIN]], axis=1)
    kvc_ref[...] = kvc
    slcb_ref[...] = kvc[:, 2 * SLOT:].astype(BF16)
    wink = normed(C_WIN, C_WIN + SLOT)
    win = jnp.concatenate([wink, z[:, C_WIN + SLOT:C_GATE]], axis=1)
    win_ref[...] = win
    winb_ref[...] = win.astype(BF16)
    gsig_ref[...] = _sigmoid(z[:, C_GATE:C_END])


def _proj(x2d, ln_g, w_re, seg, gain_row, tm):
    m, d = x2d.shape
    nt = m // tm
    row = lambda w: pl.BlockSpec((tm, w), lambda i: (i, 0))
    full = lambda a: pl.BlockSpec(a.shape, lambda i: (0,) * a.ndim)
    outs = [(384, BF16), (768, F32), (768, BF16), (256, F32), (768, BF16), (512, F32), (256, BF16),
            (256, F32), (256, BF16), (128, F32)]
    return pl.pallas_call(
        _proj_body,
        grid=(nt,),
        in_specs=[row(d), full(ln_g), full(w_re), full(seg), full(gain_row)],
        out_specs=[row(w) for w, _ in outs] + [pl.BlockSpec((1, 8, 384), lambda i: (i, 0, 0))],
        out_shape=[jax.ShapeDtypeStruct((m, w), dt) for w, dt in outs]
        + [jax.ShapeDtypeStruct((nt, 8, 384), F32)],
        compiler_params=_params(("parallel",)),
        name="proj",
    )(x2d, ln_g, w_re, seg, gain_row)


def _topk_select(score, lane_f, k, sel):
    for _ in range(k):
        mx = jnp.max(score, axis=1, keepdims=True)
        idx = jnp.min(jnp.where(score == mx, lane_f, 1e9), axis=1, keepdims=True)
        hit = lane_f == idx
        sel = sel | (hit & (mx > -jnp.inf))
        score = jnp.where(hit, -jnp.inf, score)
    return sel


def _masked_softmax_rows(s, mask):
    sm = jnp.where(mask, s, NEG_BIG)
    m = jnp.maximum(jnp.max(sm, axis=1, keepdims=True), M_INIT)
    e = jnp.where(mask, jnp.exp(sm - m), 0.0)
    l = jnp.sum(e, axis=1, keepdims=True)
    return e / jnp.where(l > 0.0, l, 1.0)


def _moba_gate_body(q_ref, km_ref, s_ref, *, pos0, tq):
    t0 = pos0 + pl.program_id(1) * tq
    lane = lax.broadcasted_iota(jnp.int32, (tq, SLOT), 1)
    lane_f = lane.astype(F32)
    own = _div_pow2(t0 + lax.broadcasted_iota(jnp.int32, (tq, 1), 0), MOBA_BLOCK)
    for h in range(N_HEADS):
        cols = slice((h // 2) * SLOT, (h // 2 + 1) * SLOT)
        q = q_ref[0, :, cols]
        mine = (lane >= HEAD_DIM) if h % 2 else (lane < HEAD_DIM)
        qh = jnp.where(mine, q, jnp.zeros_like(q))
        km_hi, km_lo = _split_bf16(km_ref[0, :, cols])
        gate = _dot_nt(qh, km_hi) + _dot_nt(qh, km_lo)
        gate = jnp.where(lane < own, gate, -jnp.inf)
        sel = _topk_select(gate, lane_f, MOBA_TOPK, lane == own)
        s_ref[0, :, h * SLOT:(h + 1) * SLOT] = jnp.where(sel, 0.0, NEG_BIG).astype(BF16)


def _moba_gate(q, kmean, pos0, tq):
    b, nq, wa = q.shape
    return pl.pallas_call(
        functools.partial(_moba_gate_body, pos0=pos0, tq=tq),
        grid=(b, nq // tq),
        in_specs=[pl.BlockSpec((1, tq, wa), lambda bi, i: (bi, i, 0)),
                  pl.BlockSpec((1, SLOT, wa), lambda bi, i: (bi, 0, 0))],
        out_specs=pl.BlockSpec((1, tq, N_HEADS * SLOT), lambda bi, i: (bi, i, 0)),
        out_shape=jax.ShapeDtypeStruct((b, nq, N_HEADS * SLOT), BF16),
        compiler_params=_params(("parallel", "arbitrary")),
        name="moba_gate",
    )(q, kmean)


def _flash_body(hd_ref, sl_ref, qa_ref, qb_ref, sa_ref, sb_ref, k_ref, v_ref, *rest, tq, nsa):
    if nsa:
        pa_ref, pb_ref, ga_ref, gb_ref, oa_ref, ob_ref, acc_ref = rest
    else:
        o_ref, acc_ref = rest
    pair = pl.program_id(1)
    i = pl.program_id(2)
    selw = sa_ref.shape[2]
    lane = lax.broadcasted_iota(jnp.int32, (tq, SLOT), 1)
    lower = lane < HEAD_DIM
    lane_k = lax.broadcasted_iota(jnp.int32, (KV_TILE, SLOT), 1)
    klower = lane_k < HEAD_DIM
    row_kf = lax.broadcasted_iota(jnp.int32, (KV_TILE, SLOT), 0).astype(F32)
    lane_s = lax.broadcasted_iota(jnp.int32, (KV_TILE, selw), 1)
    blk_of_row = _div_pow2(lax.broadcasted_iota(jnp.int32, (KV_TILE, selw), 0), SLC_BLK)
    causal = lax.broadcasted_iota(jnp.int32, (tq, KV_TILE), 1) <= lax.broadcasted_iota(jnp.int32, (tq, KV_TILE), 0)
    ones = jnp.ones((KV_TILE, SLOT), BF16)

    qaug, slope, kpos, mine_k = [], [], [], []
    for x, (q_ref, s_ref) in enumerate(((qa_ref, sa_ref), (qb_ref, sb_ref))):
        h = hd_ref[pair, x]
        base = 0 if x else HEAD_DIM
        coef = jnp.zeros((tq, SLOT), F32)
        for j in range(N_COEF):
            coef = jnp.where(lane == base + j, sl_ref[h, j], coef)
        mine = (lane >= HEAD_DIM) if x else lower
        qx = jnp.where(mine, q_ref[0].astype(F32), coef).astype(BF16)
        qaug.append(jnp.concatenate([qx, s_ref[0]], axis=1))
        slope.append(sl_ref[h, N_COEF])
        in_coef = (lane_k >= base) & (lane_k < base + N_COEF)
        kpos.append(jnp.where(in_coef, row_kf, 0.0).astype(BF16))
        mine_k.append((lane_k >= HEAD_DIM) if x else klower)
        acc_ref[x] = jnp.zeros((tq, SLOT), F32)

    def step(n, ms, masked):
        off = pl.multiple_of(n * KV_TILE, KV_TILE)
        k = k_ref[0, pl.ds(off, KV_TILE), :]
        v = v_ref[0, pl.ds(off, KV_TILE), :]
        block = n * (KV_TILE // SLC_BLK) + blk_of_row if nsa else n
        onehot = (lane_s == block).astype(BF16)
        tile_off = ((n - i) * KV_TILE).astype(F32)
        out = []
        for x in range(2):
            kaug = jnp.concatenate([jnp.where(mine_k[x], k, kpos[x]), onehot], axis=1)
            u = _dot_nt(qaug[x], kaug)
            if masked:
                u = jnp.where(causal, u, NEG_BIG)
            beta = slope[x] * tile_off
            m_new = jnp.maximum(ms[x], jnp.max(u, axis=1, keepdims=True) + beta)
            alpha = jnp.exp(ms[x] - m_new)
            p = jnp.exp(u - (m_new - beta))
            acc_ref[x] = alpha * acc_ref[x] + _dot(p.astype(BF16), jnp.where(mine_k[x], v, ones))
            out.append(m_new)
        return tuple(out)

    m0 = jnp.full((tq, 1), M_INIT, F32)
    ms = lax.fori_loop(0, i, lambda n, ms: step(n, ms, False), (m0, m0))
    step(i, ms, True)

    acc_a = acc_ref[0]
    acc_b = acc_ref[1]
    o_a = acc_a / acc_a[:, HEAD_DIM:HEAD_DIM + 1]
    o_b = acc_b / acc_b[:, 0:1]
    if nsa:
        oa_ref[0] = (pa_ref[0] + ga_ref[0] * o_a).astype(BF16)
        ob_ref[0] = (pb_ref[0] + gb_ref[0] * o_b).astype(BF16)
    else:
        o_ref[0] = jnp.where(lower, o_a, o_b).astype(BF16)


def _flash_moba(q, selb, kvb, heads, sl_tab):
    b, t, wa = q.shape
    npair = wa // SLOT
    tq = KV_TILE
    slot = lambda f: pl.BlockSpec((1, tq, SLOT), lambda bi, p, i, hd, sl: (bi, i, f(p)))
    kv = lambda f: pl.BlockSpec((1, t, SLOT), lambda bi, p, i, hd, sl: (bi, 0, f(p)))
    grid_spec = pltpu.PrefetchScalarGridSpec(
        num_scalar_prefetch=2,
        grid=(b, npair, t // tq),
        in_specs=[slot(lambda p: p), slot(lambda p: p), slot(lambda p: 2 * p), slot(lambda p: 2 * p + 1),
                  kv(lambda p: p), kv(lambda p: npair + p)],
        out_specs=slot(lambda p: p),
        scratch_shapes=[pltpu.VMEM((2, tq, SLOT), F32)],
    )
    return pl.pallas_call(
        functools.partial(_flash_body, tq=tq, nsa=False),
        grid_spec=grid_spec,
        out_shape=jax.ShapeDtypeStruct((b, t, wa), BF16),
        compiler_params=_params(("parallel", "parallel", "arbitrary")),
        name="flash_moba",
    )(heads, sl_tab, q, q, selb, selb, kvb, kvb)


def _flash_nsa(qcx, selb, slcb, part, g1x, heads, sl_tab):
    b, t, wq = qcx.shape
    npair = wq // SLOT // 2
    nsp = selb.shape[2] // 2
    tq = KV_TILE
    slot = lambda f: pl.BlockSpec((1, tq, SLOT), lambda bi, p, i, hd, sl: (bi, i, f(p)))
    sel = lambda g: pl.BlockSpec((1, tq, nsp), lambda bi, p, i, hd, sl: (bi, i, g))
    kv = lambda c: pl.BlockSpec((1, t, SLOT), lambda bi, p, i, hd, sl: (bi, 0, c))
    lo, hi = (lambda p: p), (lambda p: npair + p)
    grid_spec = pltpu.PrefetchScalarGridSpec(
        num_scalar_prefetch=2,
        grid=(b, npair, t // tq),
        in_specs=[slot(lo), slot(hi), sel(0), sel(1), kv(0), kv(1), slot(lo), slot(hi), slot(lo), slot(hi)],
        out_specs=[slot(lo), slot(lo)],
        scratch_shapes=[pltpu.VMEM((2, tq, SLOT), F32)],
    )
    return pl.pallas_call(
        functools.partial(_flash_body, tq=tq, nsa=True),
        grid_spec=grid_spec,
        out_shape=[jax.ShapeDtypeStruct((b, t, npair * SLOT), BF16)] * 2,
        compiler_params=_params(("parallel", "parallel", "arbitrary")),
        name="flash_nsa",
    )(heads, sl_tab, qcx, qcx, selb, selb, slcb, slcb, part, part, g1x, g1x)


def _decode_body(q_ref, s_ref, slope_ref, k_ref, v_ref, *rest, n_last, nsa):
    if nsa:
        part_ref, g1_ref, o_ref, acc_ref = rest
    else:
        o_ref, acc_ref = rest
    rows = q_ref.shape[1]
    selw = s_ref.shape[2]
    qaug = jnp.concatenate([q_ref[0], s_ref[0]], axis=1)
    slope = slope_ref[:, 0:1]
    lane_s = lax.broadcasted_iota(jnp.int32, (KV_TILE, selw), 1)
    row_s = lax.broadcasted_iota(jnp.int32, (KV_TILE, selw), 0)
    row_sf = row_s.astype(F32)
    is_coef = lane_s >= selw - N_COEF
    blk_of_row = _div_pow2(row_s, SLC_BLK)
    token = lax.broadcasted_iota(jnp.int32, (rows, KV_TILE), 0) & (DEC_ROWS - 1)
    causal = lax.broadcasted_iota(jnp.int32, (rows, KV_TILE), 1) <= token
    acc_ref[...] = jnp.zeros(acc_ref.shape, F32)

    def step(n, carry, masked):
        m, l = carry
        n = jnp.asarray(n, jnp.int32)
        off = pl.multiple_of(n * KV_TILE, KV_TILE)
        block = n * (KV_TILE // SLC_BLK) + blk_of_row if nsa else n
        ext = jnp.where(is_coef, row_sf, (lane_s == block).astype(F32)).astype(BF16)
        u = _dot_nt(qaug, jnp.concatenate([k_ref[0, pl.ds(off, KV_TILE), :], ext], axis=1))
        if masked:
            u = jnp.where(causal, u, NEG_BIG)
        beta = slope * ((n - n_last) * KV_TILE).astype(F32)
        m_new = jnp.maximum(m, jnp.max(u, axis=1, keepdims=True) + beta)
        alpha = jnp.exp(m - m_new)
        p = jnp.exp(u - (m_new - beta))
        acc_ref[...] = alpha * acc_ref[...] + _dot(p.astype(BF16), v_ref[0, pl.ds(off, KV_TILE), :])
        return m_new, alpha * l + jnp.sum(p, axis=1, keepdims=True)

    init = (jnp.full((rows, 1), M_INIT, F32), jnp.zeros((rows, 1), F32))
    carry = lax.fori_loop(0, n_last, lambda n, c: step(n, c, False), init)
    _, l = step(n_last, carry, True)
    o = acc_ref[...] / l
    if nsa:
        o = part_ref[0] + g1_ref[0] * o
    o_ref[0] = o


def _decode(q_rows, s_rows, slope_rows, kv, k_col, v_col, width, extra, n_last, name):
    b, rows, wq = q_rows.shape
    selw = s_rows.shape[2]
    l_pad = kv.shape[1]
    per_b = lambda w: pl.BlockSpec((1, rows, w), lambda bi: (bi, 0, 0))
    in_specs = [per_b(wq), per_b(selw), pl.BlockSpec(slope_rows.shape, lambda bi: (0, 0)),
                pl.BlockSpec((1, l_pad, width), lambda bi: (bi, 0, k_col)),
                pl.BlockSpec((1, l_pad, width), lambda bi: (bi, 0, v_col))] + [per_b(width) for _ in extra]
    return pl.pallas_call(
        functools.partial(_decode_body, n_last=n_last, nsa=bool(extra)),
        grid=(b,),
        in_specs=in_specs,
        out_specs=per_b(width),
        out_shape=jax.ShapeDtypeStruct((b, rows, width), F32),
        scratch_shapes=[pltpu.VMEM((rows, width), F32)],
        compiler_params=_params(("parallel",)),
        name=name,
    )(q_rows, s_rows, slope_rows, kv, kv, *extra)


def _cmp1_body(rawk_ref, rawv_ref, pos_ref, w_ref, u_ref, *, rows):
    nc = rows // CMP_STRIDE
    acc = [jnp.zeros((nc, SLOT), F32) for _ in range(4)]
    for r in range(CMP_STRIDE):
        for kv, raw_ref in enumerate((rawk_ref, rawv_ref)):
            xs = raw_ref[0, pl.ds(r, nc, stride=CMP_STRIDE), :]
            for half in range(2):
                rr = half * CMP_STRIDE + r
                xp = (xs + pos_ref[rr:rr + 1, kv * SLOT:(kv + 1) * SLOT]).astype(BF16)
                acc[2 * kv + half] = acc[2 * kv + half] + _dot(xp, w_ref[kv, half, r])
    u_ref[0] = jnp.concatenate(acc, axis=1)


def _cmp1(raw, pos_e, w1e, rows):
    b, l_pad, _ = raw.shape
    nc = rows // CMP_STRIDE
    return pl.pallas_call(
        functools.partial(_cmp1_body, rows=rows),
        grid=(b, l_pad // rows),
        in_specs=[
            pl.BlockSpec((1, rows, SLOT), lambda bi, i: (bi, i, 0)),
            pl.BlockSpec((1, rows, SLOT), lambda bi, i: (bi, i, 1)),
            pl.BlockSpec(pos_e.shape, lambda bi, i: (0, 0)),
            pl.BlockSpec(w1e.shape, lambda bi, i: (0, 0, 0, 0, 0)),
        ],
        out_specs=pl.BlockSpec((1, nc, 4 * SLOT), lambda bi, i: (bi, i, 0)),
        out_shape=jax.ShapeDtypeStruct((b, l_pad // CMP_STRIDE, 4 * SLOT), F32),
        compiler_params=_params(("parallel", "parallel")),
        name="cmp1",
    )(raw, raw, pos_e, w1e)


def _cmp2_body(ua_ref, ub_ref, w2_ref, seg_ref, gain_ref, kc_ref, vc_ref):
    ua = ua_ref[0]
    ub = ub_ref[0]
    pre_k = ua[:, 0:SLOT] + ub[:, SLOT:2 * SLOT]
    pre_v = ua[:, 2 * SLOT:3 * SLOT] + ub[:, 3 * SLOT:4 * SLOT]
    ck = _dot(jax.nn.gelu(pre_k).astype(BF16), w2_ref[0])
    cv = _dot(jax.nn.gelu(pre_v).astype(BF16), w2_ref[1])
    kc_ref[0] = _head_rms(ck, seg_ref[...], gain_ref[...]).astype(BF16)
    vc_ref[0] = cv.astype(BF16)


def _cmp2(ua, ub, w2e, seg, gain3):
    b, ncp, _ = ua.shape
    blk = pl.BlockSpec((1, ncp, 4 * SLOT), lambda bi: (bi, 0, 0))
    out = pl.BlockSpec((1, ncp, SLOT), lambda bi: (bi, 0, 0))
    return pl.pallas_call(
        _cmp2_body,
        grid=(b,),
        in_specs=[blk, blk, pl.BlockSpec(w2e.shape, lambda bi: (0, 0, 0)),
                  pl.BlockSpec(seg.shape, lambda bi: (0, 0)), pl.BlockSpec(gain3.shape, lambda bi: (0, 0))],
        out_specs=[out, out],
        out_shape=[jax.ShapeDtypeStruct((b, ncp, SLOT), BF16)] * 2,
        compiler_params=_params(("parallel",)),
        name="cmp2",
    )(ua, ub, w2e, seg, gain3)


def _nsa_sel_body(sl_ref, q_ref, g_ref, kc_ref, vc_ref, ov_ref, wk_ref, wv_ref, part_ref, g1_ref, selb_ref, *,
                  pos0, tq, n_cmp, pw0, lw_valid):
    t0 = pos0 + pl.program_id(1) * tq
    q = q_ref[0]
    gs = g_ref[0]
    ncp = kc_ref.shape[1]
    nsp = ov_ref.shape[1]
    n_wtiles = wk_ref.shape[1] // KV_TILE
    t = t0 + lax.broadcasted_iota(jnp.int32, (tq, 1), 0)
    t3 = jnp.concatenate([t, t, t], axis=0)
    lane = lax.broadcasted_iota(jnp.int32, (tq, SLOT), 1)
    mcol = lax.broadcasted_iota(jnp.int32, (tq, nsp), 1)
    mcol_f = mcol.astype(F32)
    col = lax.broadcasted_iota(jnp.int32, (1, KV_TILE), 1)
    own = _div_pow2(t, SLC_BLK)

    parts, g1s, selbs = [], [], []
    for g in range(2):
        q3 = jnp.concatenate([q[:, (3 * g + r) * SLOT:(3 * g + r + 1) * SLOT] for r in range(3)], axis=0)
        slope3 = jnp.concatenate([jnp.full((tq, 1), sl_ref[3 * g + r], F32) for r in range(3)], axis=0)

        cend = lax.broadcasted_iota(jnp.int32, (1, ncp), 1) * CMP_STRIDE + (CMP_BLK - 1)
        dist = t3 - cend
        s = _dot_nt(q3, kc_ref[0]) - slope3 * dist.astype(F32)
        mask = (dist >= 0) & (cend < n_cmp * CMP_STRIDE + (CMP_BLK - 1))
        p = _masked_softmax_rows(s, mask)
        o_cmp = _dot(p.astype(BF16), vc_ref[0])
        p_hi, p_lo = _split_bf16(p[0:tq] + p[tq:2 * tq] + p[2 * tq:3 * tq])
        imp = _dot(p_hi, ov_ref[...]) + _dot(p_lo, ov_ref[...])

        forced = (mcol == 0) | (mcol == own) | (mcol == own - 1)
        score = jnp.where(forced, FORCE_SCORE, jnp.where(mcol <= own, imp, -jnp.inf))
        sel = _topk_select(score, mcol_f, SLC_TOPK, jnp.zeros((tq, nsp), jnp.bool_))
        selbs.append(jnp.where(sel, 0.0, NEG_BIG).astype(BF16))

        wb = (t0 - pw0) // KV_TILE - 2
        s_parts, v_parts, m_parts = [], [], []
        for j in range(3):
            tile = wb + j
            off = pl.multiple_of(jnp.clip(tile, 0, n_wtiles - 1) * KV_TILE, KV_TILE)
            ridx = tile * KV_TILE + col
            dist = t3 - (pw0 + ridx)
            s_parts.append(_dot_nt(q3, wk_ref[0, pl.ds(off, KV_TILE), :]) - slope3 * dist.astype(F32))
            m_parts.append((ridx >= 0) & (ridx < lw_valid) & (dist >= 0) & (dist <= WINDOW))
            v_parts.append(wv_ref[0, pl.ds(off, KV_TILE), :])
        p = _masked_softmax_rows(jnp.concatenate(s_parts, axis=1), jnp.concatenate(m_parts, axis=1))
        o_win = sum(_dot(p[:, j * KV_TILE:(j + 1) * KV_TILE].astype(BF16), v_parts[j]) for j in range(3))

        in_group = (lane >= HEAD_DIM) if g else (lane < HEAD_DIM)
        for r in range(3):
            hd = 3 * g + r
            rows = slice(r * tq, (r + 1) * tq)
            o = gs[:, 3 * hd:3 * hd + 1] * o_cmp[rows] + gs[:, 3 * hd + 2:3 * hd + 3] * o_win[rows]
            parts.append(jnp.where(in_group, o, 0.0))
            g1s.append(jnp.where(in_group, gs[:, 3 * hd + 1:3 * hd + 2], 0.0))
    part_ref[0] = jnp.concatenate(parts, axis=1)
    g1_ref[0] = jnp.concatenate(g1s, axis=1)
    selb_ref[0] = jnp.concatenate(selbs, axis=1)


def _nsa_sel(qcx, gsig, kc, vc, ov, winb, slopes, pos0, tq, n_cmp, pw0, lw_valid):
    b, nq, wq = qcx.shape
    lw = winb.shape[1]
    ncp = kc.shape[1]
    nsp = ov.shape[1]
    grid_spec = pltpu.PrefetchScalarGridSpec(
        num_scalar_prefetch=1,
        grid=(b, nq // tq),
        in_specs=[
            pl.BlockSpec((1, tq, wq), lambda bi, i, sl: (bi, i, 0)),
            pl.BlockSpec((1, tq, SLOT), lambda bi, i, sl: (bi, i, 0)),
            pl.BlockSpec((1, ncp, SLOT), lambda bi, i, sl: (bi, 0, 0)),
            pl.BlockSpec((1, ncp, SLOT), lambda bi, i, sl: (bi, 0, 0)),
            pl.BlockSpec(ov.shape, lambda bi, i, sl: (0, 0)),
            pl.BlockSpec((1, lw, SLOT), lambda bi, i, sl: (bi, 0, 0)),
            pl.BlockSpec((1, lw, SLOT), lambda bi, i, sl: (bi, 0, 1)),
        ],
        out_specs=[pl.BlockSpec((1, tq, wq), lambda bi, i, sl: (bi, i, 0)),
                   pl.BlockSpec((1, tq, wq), lambda bi, i, sl: (bi, i, 0)),
                   pl.BlockSpec((1, tq, 2 * nsp), lambda bi, i, sl: (bi, i, 0))],
    )
    return pl.pallas_call(
        functools.partial(_nsa_sel_body, pos0=pos0, tq=tq, n_cmp=n_cmp, pw0=pw0, lw_valid=lw_valid),
        grid_spec=grid_spec,
        out_shape=[jax.ShapeDtypeStruct((b, nq, wq), F32), jax.ShapeDtypeStruct((b, nq, wq), F32),
                   jax.ShapeDtypeStruct((b, nq, 2 * nsp), BF16)],
        compiler_params=_params(("parallel", "arbitrary")),
        name="nsa_sel",
    )(slopes, qcx, gsig, kc, vc, ov, winb, winb)


def _conv_body(cin_ref, cw_ref, cb_ref, lg_ref, lb_ref, y_ref, *, tm):
    off = pl.multiple_of(pl.program_id(1) * tm, 8)
    win = cin_ref[0, pl.ds(off, tm + CONV_PAD), :]
    acc = jnp.zeros((tm, win.shape[1]), F32)
    for w in range(CONV_W):
        lo = w + CONV_PAD - (CONV_W - 1)
        acc = acc + cw_ref[w:w + 1, :] * win[lo:lo + tm, :]
    acc = acc + cb_ref[...]
    mu = jnp.mean(acc, axis=-1, keepdims=True)
    cen = acc - mu
    var = jnp.mean(cen * cen, axis=-1, keepdims=True)
    yn = cen * lax.rsqrt(var + EPS) * lg_ref[...] + lb_ref[...]
    y_ref[0] = (yn * _sigmoid(yn)).astype(BF16)


def _conv(cin, cw, cb, lg, lb, tm):
    b, rows, ch = cin.shape
    t_pad = rows - CONV_PAD
    full2 = lambda a: pl.BlockSpec(a.shape, lambda bi, i: (0, 0))
    return pl.pallas_call(
        functools.partial(_conv_body, tm=tm),
        grid=(b, t_pad // tm),
        in_specs=[pl.BlockSpec((1, rows, ch), lambda bi, i: (bi, 0, 0)), full2(cw), full2(cb), full2(lg), full2(lb)],
        out_specs=pl.BlockSpec((1, tm, ch), lambda bi, i: (bi, i, 0)),
        out_shape=jax.ShapeDtypeStruct((b, t_pad, ch), BF16),
        compiler_params=_params(("parallel", "arbitrary")),
        name="conv",
    )(cin, cw, cb, lg, lb)


def _ffn_body(x_ref, oa_ref, y_ref, oc0_ref, oc1_ref, wa_ref, wy_ref, wc0_ref, wc1_ref, g2_ref, wu_ref, wd_ref,
              out_ref):
    x1 = (x_ref[...] + _dot(oa_ref[...], wa_ref[...]) + _dot(y_ref[...], wy_ref[...])
          + _dot(oc0_ref[...], wc0_ref[...]) + _dot(oc1_ref[...], wc1_ref[...]))
    ms = jnp.mean(x1 * x1, axis=-1, keepdims=True)
    h2 = (x1 * lax.rsqrt(ms + EPS) * g2_ref[...]).astype(BF16)
    u = jnp.maximum(_dot(h2, wu_ref[...]), 0.0)
    out_ref[...] = x1 + _dot((u * u).astype(BF16), wd_ref[...])


def _ffn(x2d, acts, weights, tm):
    m, d = x2d.shape
    row = lambda a: pl.BlockSpec((tm, a.shape[1]), lambda i: (i, 0))
    const = lambda a: pl.BlockSpec(a.shape, lambda i: (0, 0), pipeline_mode=pl.Buffered(1))
    return pl.pallas_call(
        _ffn_body,
        grid=(m // tm,),
        in_specs=[row(x2d)] + [row(a) for a in acts] + [const(w) for w in weights],
        out_specs=pl.BlockSpec((tm, d), lambda i: (i, 0)),
        out_shape=jax.ShapeDtypeStruct((m, d), F32),
        compiler_params=_params(("parallel",)),
        name="out_ffn",
    )(x2d, *acts, *weights)


def _gather_moba_body(pt_ref, p0_ref, p1_ref, tail_ref, kvb_ref, km_ref, *, n_past):
    blk = jnp.concatenate([p0_ref[0], p1_ref[0]], axis=0)
    blk = jnp.where(pl.program_id(1) < n_past, blk, tail_ref[0])
    kvb_ref[0] = blk.astype(BF16)
    wk = km_ref.shape[-1]
    km_ref[0, 0] = jnp.sum(blk[:, :wk], axis=0, keepdims=True) * (1.0 / MOBA_BLOCK)


def _gather_nsa_body(pt_ref, p0_ref, p1_ref, tail_ref, raw_ref, slcb_ref, *, n_past):
    blk = jnp.concatenate([p0_ref[0], p1_ref[0]], axis=0)
    blk = jnp.where(pl.program_id(1) < n_past, blk, tail_ref[0])
    raw_ref[0] = blk[:, :2 * SLOT]
    slcb_ref[0] = blk[:, 2 * SLOT:].astype(BF16)


def _gather(body, name, cache, tail, page_table, out_widths, out_dtypes, km_width=None):
    bsz, n_pages = page_table.shape
    w = cache.shape[-1]
    n_past = n_pages * PAGE_SIZE // KV_TILE
    nb = n_past + 1

    def page(k):
        return pl.BlockSpec((1, PAGE_SIZE, w),
                            lambda bi, n, pt: (pt[bi, jnp.minimum(2 * n + k, n_pages - 1)], 0, 0))

    out_specs = [pl.BlockSpec((1, KV_TILE, ow), lambda bi, n, pt: (bi, n, 0)) for ow in out_widths]
    out_shape = [jax.ShapeDtypeStruct((bsz, nb * KV_TILE, ow), dt) for ow, dt in zip(out_widths, out_dtypes)]
    if km_width is not None:
        out_specs.append(pl.BlockSpec((1, 1, 1, km_width), lambda bi, n, pt: (bi, n, 0, 0)))
        out_shape.append(jax.ShapeDtypeStruct((bsz, nb, 1, km_width), F32))
    grid_spec = pltpu.PrefetchScalarGridSpec(
        num_scalar_prefetch=1,
        grid=(bsz, nb),
        in_specs=[page(0), page(1), pl.BlockSpec((1, KV_TILE, w), lambda bi, n, pt: (bi, 0, 0))],
        out_specs=out_specs,
    )
    return pl.pallas_call(
        functools.partial(body, n_past=n_past),
        grid_spec=grid_spec,
        out_shape=out_shape,
        compiler_params=_params(("parallel", "arbitrary")),
        name=name,
    )(page_table, cache, cache, tail)


def _layer_consts(lp):
    (ln1_g, w_in, qk_gain, cmp_pos, cmp_w1, cmp_w2, conv_w, conv_b, conv_ln_g, conv_ln_b, w_out, ln2_g, w_up,
     w_down) = lp
    d = w_in.shape[0]
    z64 = jnp.zeros((d, HEAD_DIM), F32)
    qc = w_in[:, 1664:2048].reshape(d, 6, HEAD_DIM)
    qc_slots = [jnp.concatenate([qc[:, h], z64] if h < 3 else [z64, qc[:, h]], axis=1) for h in range(6)]
    gate = jnp.pad(w_in[:, 2816:2834], ((0, 0), (0, SLOT - 18)))
    w_re = jnp.concatenate([w_in[:, 0:1664]] + qc_slots + [w_in[:, 2048:2816], gate], axis=1).astype(BF16)

    ones = lambda n: jnp.ones((n,), F32)
    g = qk_gain
    gain_row = jnp.concatenate([
        jnp.tile(g[0], 6) * SCALE, jnp.tile(g[1], 6), ones(C_QC - C_VA),
        jnp.tile(g[2], 12) * SCALE, ones(2 * SLOT), jnp.tile(g[4], 2), ones(SLOT),
        jnp.tile(g[5], 2), ones(SLOT), ones(SLOT)])[None, :]
    seg = jnp.asarray(np.kron(np.eye(2, dtype=np.float32), np.full((HEAD_DIM, HEAD_DIM), 1.0 / HEAD_DIM, np.float32)),
                      BF16)

    eye2 = jnp.eye(2, dtype=F32)
    w1 = cmp_w1.reshape(2, 2, CMP_STRIDE, HEAD_DIM, HEAD_DIM)
    w1e = jnp.einsum("ab,khrdj->khradbj", eye2, w1).reshape(2, 2, CMP_STRIDE, SLOT, SLOT).astype(BF16)
    w2e = jnp.einsum("ab,kdj->kadbj", eye2, cmp_w2).reshape(2, SLOT, SLOT).astype(BF16)
    pos_e = jnp.concatenate([jnp.tile(cmp_pos[0], (1, 2)), jnp.tile(cmp_pos[1], (1, 2))], axis=1)
    gain3 = jnp.tile(g[3], 2)[None, :]

    wo = w_out
    wa = wo[0:384].astype(BF16)
    wy = wo[384:640].astype(BF16)
    wc_rows = wo[640:1024].reshape(6, HEAD_DIM, d)
    zr = jnp.zeros((HEAD_DIM, d), F32)
    wc = jnp.concatenate([jnp.concatenate([wc_rows[h], zr] if h < 3 else [zr, wc_rows[h]], axis=0)
                          for h in range(6)], axis=0).astype(BF16)
    ffn_w = (wa, wy, wc[:3 * SLOT], wc[3 * SLOT:], ln2_g[None, :], w_up.astype(BF16), w_down.astype(BF16))
    return dict(ln1=ln1_g[None, :], w_re=w_re, gain_row=gain_row, seg=seg, w1e=w1e, w2e=w2e, pos_e=pos_e,
                gain3=gain3, conv_w=conv_w, conv_b=conv_b[None, :], conv_lg=conv_ln_g[None, :],
                conv_lb=conv_ln_b[None, :], ffn_w=ffn_w)


def _overlap_matrix(ncp, nsp, n_cmp, n_slc):
    n = np.arange(ncp)[:, None]
    m = np.arange(nsp)[None, :]
    ov = ((m == (n * CMP_STRIDE) // SLC_BLK) | (m == (n * CMP_STRIDE + CMP_BLK - 1) // SLC_BLK))
    ov = ov & (n < n_cmp) & (m < n_slc)
    return jnp.asarray(ov.astype(np.float32), BF16)


def _round_up(x, m):
    return -(-x // m) * m


def _pick_rows(l_pad):
    for rows in (2048, 1280, 1024, 768, 512, 256):
        if l_pad % rows == 0:
            return rows
    raise ValueError(l_pad)


def _head_rows(a, t_rows):
    b = a.shape[0]
    w = a.shape[2] // N_HEADS
    return a.reshape(b, t_rows, N_HEADS, w).transpose(0, 2, 1, 3).reshape(b, N_HEADS * t_rows, w)


def _with_coefs(s_rows, sl_tab, n_blocks):
    selw = s_rows.shape[2]
    assert n_blocks <= selw - N_COEF
    coef = np.zeros((N_HEADS * DEC_ROWS, selw), np.float32)
    coef[:, selw - N_COEF:] = np.repeat(sl_tab[:, :N_COEF], DEC_ROWS, axis=0)
    lane = np.arange(selw)[None, :] >= selw - N_COEF
    return jnp.where(jnp.asarray(lane)[None], jnp.asarray(coef, BF16)[None], s_rows)


def _layer(x, c, past, tabs):
    b, t, d = x.shape
    m = b * t
    tm = 256 if m % 256 == 0 else m
    sl_np = tabs["sl_np"]
    qa, kva, kvab, glu, qcx, kvc, slcb, win, winb, gsig, kmean_t = _proj(
        x.reshape(m, d), c["ln1"], c["w_re"], c["seg"], c["gain_row"], tm)
    r3 = lambda a: a.reshape(b, t, a.shape[-1])
    qa, kva, kvab, glu, qcx, kvc, slcb, win, winb, gsig = map(r3, (qa, kva, kvab, glu, qcx, kvc, slcb, win, winb, gsig))
    decode = past is not None

    if not decode:
        past_len = 0
        tq = KV_TILE
        assert t % KV_TILE == 0
        l_pad = t
        nb = t // MOBA_BLOCK
        kmean = kmean_t[:, 0, :].reshape(b, nb, 384)
        raw = kvc
        win_all = winb
        lw_valid = t
        pw0 = 0
        q_a, q_c, g_c = qa, qcx, gsig
        new_win_state = win[:, -min(WINDOW, t):]
        conv_in = jnp.pad(glu, ((0, 0), (CONV_PAD, 0), (0, 0)))
        new_conv = glu[:, -(CONV_W - 1):]
        conv_tm = 256
    else:
        page_table = past["page_table"]
        past_len = page_table.shape[1] * PAGE_SIZE
        assert past_len % KV_TILE == 0 and t <= DEC_ROWS
        tq = DEC_ROWS
        l_pad = past_len + KV_TILE
        nb = l_pad // MOBA_BLOCK
        padt = lambda a: jnp.pad(a, ((0, 0), (0, KV_TILE - t), (0, 0)))
        kvab, km4 = _gather(_gather_moba_body, "gather_moba", past["moba"], padt(kva), page_table,
                            [768], [BF16], km_width=384)
        kmean = km4[:, :, 0, :]
        raw, slcb = _gather(_gather_nsa_body, "gather_nsa", past["nsa"], padt(kvc), page_table,
                            [2 * SLOT, 2 * SLOT], [F32, BF16])
        win_buf = past["win"]
        nbuf = win_buf.shape[1]
        lw_valid = nbuf + t
        win_cat = jnp.concatenate([win_buf, win], axis=1)
        win_all = jnp.pad(win_cat, ((0, 0), (0, _round_up(lw_valid, KV_TILE) - lw_valid), (0, 0))).astype(BF16)
        pw0 = past_len - nbuf
        padq = lambda a: jnp.pad(a, ((0, 0), (0, tq - t), (0, 0)))
        q_a, q_c, g_c = padq(qa), padq(qcx), padq(gsig)
        new_win_state = win_cat[:, -nbuf:]
        conv_cat = jnp.concatenate([past["conv"], glu], axis=1)
        conv_in = jnp.pad(conv_cat, ((0, 0), (CONV_PAD - (CONV_W - 1), 8 - t), (0, 0)))
        new_conv = conv_cat[:, -(CONV_W - 1):]
        conv_tm = 8

    n_cmp = l_pad // CMP_STRIDE - 1
    n_slc = l_pad // SLC_BLK
    ncp = _round_up(n_cmp, SLOT)
    nsp = _round_up(n_slc + (N_COEF if decode else 0), SLOT)
    u = _cmp1(raw, c["pos_e"], c["w1e"], _pick_rows(l_pad))
    ua = jnp.pad(u[:, :n_cmp], ((0, 0), (0, ncp - n_cmp), (0, 0)))
    ub = jnp.pad(u[:, 1:], ((0, 0), (0, ncp - n_cmp), (0, 0)))
    kc, vc = _cmp2(ua, ub, c["w2e"], c["seg"], c["gain3"])
    ov = _overlap_matrix(ncp, nsp, n_cmp, n_slc)
    part, g1x, selb_c = _nsa_sel(q_c, g_c, kc, vc, ov, win_all, tabs["slopes"], past_len, tq, n_cmp, pw0, lw_valid)

    kmean = jnp.pad(kmean, ((0, 0), (0, SLOT - nb), (0, 0)))
    selb_a = _moba_gate(q_a, kmean, past_len, tq)

    if not decode:
        o_a = _flash_moba(q_a, selb_a, kvab, tabs["heads_a"], tabs["sl_tab"])
        oc0, oc1 = _flash_nsa(q_c, selb_c, slcb, part, g1x, tabs["heads_c"], tabs["sl_tab"])
    else:
        n_last = past_len // KV_TILE
        slope_rows = jnp.asarray(np.repeat(sl_np[:, N_COEF:], DEC_ROWS, axis=0) * np.ones((1, SLOT), np.float32))
        head_of_lane = np.arange(N_HEADS * HEAD_DIM)[None, :] // HEAD_DIM == np.arange(N_HEADS)[:, None]
        q_rows = jnp.where(jnp.asarray(head_of_lane)[None, :, None, :], q_a[:, None], jnp.zeros((), BF16))
        q_rows = q_rows.reshape(b, N_HEADS * DEC_ROWS, q_a.shape[2])
        s_rows = _with_coefs(_head_rows(selb_a, DEC_ROWS), sl_np, nb)
        o_rows = _decode(q_rows, s_rows, slope_rows, kvab, 0, 1, 384, (), n_last, "decode_moba")
        o_a = jnp.concatenate([o_rows[:, h * DEC_ROWS:h * DEC_ROWS + t, h * HEAD_DIM:(h + 1) * HEAD_DIM]
                               for h in range(N_HEADS)], axis=2).astype(BF16)
        sel_g = selb_c.reshape(b, DEC_ROWS, 2, nsp)
        s_rows = jnp.concatenate([sel_g[:, :, h // 3] for h in range(N_HEADS)], axis=1)
        s_rows = _with_coefs(s_rows, sl_np, n_slc)
        o_rows = _decode(_head_rows(q_c, DEC_ROWS), s_rows, slope_rows, slcb, 0, 1, SLOT,
                         (_head_rows(part, DEC_ROWS), _head_rows(g1x, DEC_ROWS)), n_last, "decode_nsa")
        o_c = o_rows.reshape(b, N_HEADS, DEC_ROWS, SLOT).transpose(0, 2, 1, 3).reshape(b, DEC_ROWS, N_HEADS * SLOT)
        o_c = o_c[:, :t].astype(BF16)
        oc0, oc1 = o_c[:, :, :3 * SLOT], o_c[:, :, 3 * SLOT:]

    y = _conv(conv_in, c["conv_w"], c["conv_b"], c["conv_lg"], c["conv_lb"], conv_tm)[:, :t]

    f2 = lambda a: a.reshape(m, a.shape[-1])
    x_new = _ffn(x.reshape(m, d), (f2(o_a), f2(y), f2(oc0), f2(oc1)), c["ffn_w"], tm)
    return x_new.reshape(b, t, d), kva, kvc, new_win_state, new_conv


def _tables():
    sl_np = _slope_table(N_HEADS)
    return dict(sl_np=sl_np, sl_tab=jnp.asarray(sl_np), slopes=jnp.asarray(sl_np[:, N_COEF]),
                heads_a=jnp.asarray(np.array([[0, 1], [2, 3], [4, 5]], np.int32)),
                heads_c=jnp.asarray(np.array([[0, 3], [1, 4], [2, 5]], np.int32)))


def kernel(x_prompt, x_sample, cache_moba_kv, cache_nsa_kv, cache_win_kv, cache_conv, page_table, ln1_g, w_in,
           qk_gain, cmp_pos, cmp_w1, cmp_w2, conv_w, conv_b, conv_ln_g, conv_ln_b, w_out, ln2_g, w_up, w_down):
    depth = w_in.shape[0]
    bp, tp, _ = x_prompt.shape
    bs, ts, _ = x_sample.shape
    n_pool = cache_moba_kv.shape[1]
    tabs = _tables()
    yp, ys = x_prompt, x_sample
    outs = [[] for _ in range(8)]
    for l in range(depth):
        c = _layer_consts((ln1_g[l], w_in[l], qk_gain[l], cmp_pos[l], cmp_w1[l], cmp_w2[l], conv_w[l], conv_b[l],
                           conv_ln_g[l], conv_ln_b[l], w_out[l], ln2_g[l], w_up[l], w_down[l]))
        yp, a, cc, w, cv = _layer(yp, c, None, tabs)
        outs[0].append(a.reshape(bp, tp, 2, 6, HEAD_DIM))
        outs[2].append(cc.reshape(bp, tp, 4, 2, HEAD_DIM))
        outs[4].append(w.reshape(bp, w.shape[1], 2, 2, HEAD_DIM))
        outs[6].append(cv)
        past = dict(page_table=page_table + l * n_pool,
                    moba=cache_moba_kv.reshape(depth * n_pool, PAGE_SIZE, 768),
                    nsa=cache_nsa_kv.reshape(depth * n_pool, PAGE_SIZE, 512),
                    win=cache_win_kv[l].reshape(bs, cache_win_kv.shape[2], 256),
                    conv=cache_conv[l])
        ys, a, cc, w, cv = _layer(ys, c, past, tabs)
        outs[1].append(a.reshape(bs, ts, 2, 6, HEAD_DIM))
        outs[3].append(cc.reshape(bs, ts, 4, 2, HEAD_DIM))
        outs[5].append(w.reshape(bs, w.shape[1], 2, 2, HEAD_DIM))
        outs[7].append(cv)
    return (yp, ys) + tuple(jnp.stack(o) for o in outs)
```

```python
import functools

import numpy as np
import jax
import jax.numpy as jnp
from jax import lax
from jax.experimental import pallas as pl
from jax.experimental.pallas import tpu as pltpu

F32 = jnp.float32
BF16 = jnp.bfloat16

HEAD_DIM = 64
SLOT = 2 * HEAD_DIM
N_HEADS = 6
PAGE_SIZE = 128
MOBA_BLOCK = 256
MOBA_TOPK = 3
CMP_STRIDE = 16
CMP_BLK = 32
SLC_BLK = 64
SLC_TOPK = 16
WINDOW = 512
CONV_W = 31
CONV_PAD = 32
EPS = 1e-6
FORCE_SCORE = 1e4
SCALE = HEAD_DIM ** -0.5
NEG_BIG = -(2.0 ** 100)
M_INIT = -(2.0 ** 99)
KV_TILE = 256
DEC_ROWS = 16
N_COEF = 3
VMEM_LIMIT = 56 * 1024 * 1024

C_QA, C_KA, C_VA, C_GLU, C_QC, C_KVC, C_WIN, C_GATE, C_END = 0, 384, 768, 1152, 1664, 2432, 2944, 3200, 3328


def _alibi_slopes(n):
    return np.array([2.0 ** (-8.0 * (i + 1) / n) for i in range(n)], dtype=np.float32)


def _slope_table(n):
    sl = _alibi_slopes(n)
    terms, rest = [], sl.copy()
    for _ in range(N_COEF):
        term = rest.astype(BF16).astype(np.float32)
        terms.append(term)
        rest = rest - term
    assert not rest.any()
    return np.stack(terms + [sl], axis=1)


def _dot(a, b):
    return jnp.dot(a, b, preferred_element_type=F32)


def _dot_nt(a, b):
    return lax.dot_general(a, b, (((1,), (1,)), ((), ())), preferred_element_type=F32)


def _split_bf16(x):
    hi = x.astype(BF16)
    lo = (x - hi.astype(F32)).astype(BF16)
    return hi, lo


def _div_pow2(x, d):
    return lax.shift_right_logical(x, jnp.int32(int(d).bit_length() - 1))


def _sigmoid(x):
    return 1.0 / (1.0 + jnp.exp(-x))


def _params(sem, vmem=VMEM_LIMIT):
    return pltpu.CompilerParams(dimension_semantics=sem, vmem_limit_bytes=vmem)


def _head_rms(zs, seg, gain):
    cols = []
    for c in range(zs.shape[1] // SLOT):
        zz = zs[:, c * SLOT:(c + 1) * SLOT]
        hi, lo = _split_bf16(zz * zz)
        cols.append(_dot(hi, seg) + _dot(lo, seg))
    ms = cols[0] if len(cols) == 1 else jnp.concatenate(cols, axis=1)
    return zs * lax.rsqrt(ms + EPS) * gain


def _proj_body(x_ref, ln_ref, w_ref, seg_ref, gain_ref,
               qa_ref, kva_ref, kvab_ref, glu_ref, qcx_ref, kvc_ref, slcb_ref, win_ref, winb_ref,
               gsig_ref, kmean_ref):
    x = x_ref[...]
    ms = jnp.mean(x * x, axis=-1, keepdims=True)
    h = (x * lax.rsqrt(ms + EPS) * ln_ref[...]).astype(BF16)
    z = _dot(h, w_ref[...])
    seg = seg_ref[...]

    def normed(lo, hi):
        return _head_rms(z[:, lo:hi], seg, gain_ref[:, lo:hi])

    qa_ref[...] = normed(C_QA, C_KA).astype(BF16)
    ka = normed(C_KA, C_VA)
    kva = jnp.concatenate([ka, z[:, C_VA:C_GLU]], axis=1)
    kva_ref[...] = kva
    kvab_ref[...] = kva.astype(BF16)
    kmean_ref[0] = jnp.broadcast_to(jnp.sum(ka, axis=0, keepdims=True) * (1.0 / ka.shape[0]), (8, ka.shape[1]))
    half = (C_QC - C_GLU) // 2
    glu_ref[...] = z[:, C_GLU:C_GLU + half] * _sigmoid(z[:, C_GLU + half:C_QC])
    qcx_ref[...] = normed(C_QC, C_KVC).astype(BF16)
    slck = normed(C_KVC + 2 * SLOT, C_KVC + 3 * SLOT)
    kvc = jnp.concatenate([z[:, C_KVC:C_KVC + 2 * SLOT], slck, z[:, C_KVC + 3 * SLOT:C_WIN]], axis=1)
    kvc_ref[...] = kvc
    slcb_ref[...] = kvc[:, 2 * SLOT:].astype(BF16)
    wink = normed(C_WIN, C_WIN + SLOT)
    win = jnp.concatenate([wink, z[:, C_WIN + SLOT:C_GATE]], axis=1)
    win_ref[...] = win
    winb_ref[...] = win.astype(BF16)
    gsig_ref[...] = _sigmoid(z[:, C_GATE:C_END])


def _proj(x2d, ln_g, w_re, seg, gain_row, tm):
    m, d = x2d.shape
    nt = m // tm
    row = lambda w: pl.BlockSpec((tm, w), lambda i: (i, 0))
    full = lambda a: pl.BlockSpec(a.shape, lambda i: (0,) * a.ndim)
    outs = [(384, BF16), (768, F32), (768, BF16), (256, F32), (768, BF16), (512, F32), (256, BF16),
            (256, F32), (256, BF16), (128, F32)]
    return pl.pallas_call(
        _proj_body,
        grid=(nt,),
        in_specs=[row(d), full(ln_g), full(w_re), full(seg), full(gain_row)],
        out_specs=[row(w) for w, _ in outs] + [pl.BlockSpec((1, 8, 384), lambda i: (i, 0, 0))],
        out_shape=[jax.ShapeDtypeStruct((m, w), dt) for w, dt in outs]
        + [jax.ShapeDtypeStruct((nt, 8, 384), F32)],
        compiler_params=_params(("parallel",)),
        name="proj",
    )(x2d, ln_g, w_re, seg, gain_row)


def _topk_select(score, lane_f, k, sel):
    for _ in range(k):
        mx = jnp.max(score, axis=1, keepdims=True)
        idx = jnp.min(jnp.where(score == mx, lane_f, 1e9), axis=1, keepdims=True)
        hit = lane_f == idx
        sel = sel | (hit & (mx > -jnp.inf))
        score = jnp.where(hit, -jnp.inf, score)
    return sel


def _masked_softmax_rows(s, mask):
    sm = jnp.where(mask, s, NEG_BIG)
    m = jnp.maximum(jnp.max(sm, axis=1, keepdims=True), M_INIT)
    e = jnp.where(mask, jnp.exp(sm - m), 0.0)
    l = jnp.sum(e, axis=1, keepdims=True)
    return e / jnp.where(l > 0.0, l, 1.0)


def _moba_gate_body(q_ref, km_ref, s_ref, *, pos0, tq):
    t0 = pos0 + pl.program_id(1) * tq
    lane = lax.broadcasted_iota(jnp.int32, (tq, SLOT), 1)
    lane_f = lane.astype(F32)
    own = _div_pow2(t0 + lax.broadcasted_iota(jnp.int32, (tq, 1), 0), MOBA_BLOCK)
    for h in range(N_HEADS):
        cols = slice((h // 2) * SLOT, (h // 2 + 1) * SLOT)
        q = q_ref[0, :, cols]
        mine = (lane >= HEAD_DIM) if h % 2 else (lane < HEAD_DIM)
        qh = jnp.where(mine, q, jnp.zeros_like(q))
        km_hi, km_lo = _split_bf16(km_ref[0, :, cols])
        gate = _dot_nt(qh, km_hi) + _dot_nt(qh, km_lo)
        gate = jnp.where(lane < own, gate, -jnp.inf)
        sel = _topk_select(gate, lane_f, MOBA_TOPK, lane == own)
        s_ref[0, :, h * SLOT:(h + 1) * SLOT] = jnp.where(sel, 0.0, NEG_BIG).astype(BF16)


def _moba_gate(q, kmean, pos0, tq):
    b, nq, wa = q.shape
    return pl.pallas_call(
        functools.partial(_moba_gate_body, pos0=pos0, tq=tq),
        grid=(b, nq // tq),
        in_specs=[pl.BlockSpec((1, tq, wa), lambda bi, i: (bi, i, 0)),
                  pl.BlockSpec((1, SLOT, wa), lambda bi, i: (bi, 0, 0))],
        out_specs=pl.BlockSpec((1, tq, N_HEADS * SLOT), lambda bi, i: (bi, i, 0)),
        out_shape=jax.ShapeDtypeStruct((b, nq, N_HEADS * SLOT), BF16),
        compiler_params=_params(("parallel", "arbitrary")),
        name="moba_gate",
    )(q, kmean)


def _flash_body(hd_ref, sl_ref, qa_ref, qb_ref, sa_ref, sb_ref, k_ref, v_ref, *rest, tq, nsa):
    if nsa:
        pa_ref, pb_ref, ga_ref, gb_ref, oa_ref, ob_ref, acc_ref = rest
    else:
        o_ref, acc_ref = rest
    pair = pl.program_id(1)
    i = pl.program_id(2)
    selw = sa_ref.shape[2]
    lane = lax.broadcasted_iota(jnp.int32, (tq, SLOT), 1)
    lower = lane < HEAD_DIM
    lane_k = lax.broadcasted_iota(jnp.int32, (KV_TILE, SLOT), 1)
    klower = lane_k < HEAD_DIM
    row_kf = lax.broadcasted_iota(jnp.int32, (KV_TILE, SLOT), 0).astype(F32)
    lane_s = lax.broadcasted_iota(jnp.int32, (KV_TILE, selw), 1)
    blk_of_row = _div_pow2(lax.broadcasted_iota(jnp.int32, (KV_TILE, selw), 0), SLC_BLK)
    causal = lax.broadcasted_iota(jnp.int32, (tq, KV_TILE), 1) <= lax.broadcasted_iota(jnp.int32, (tq, KV_TILE), 0)
    ones = jnp.ones((KV_TILE, SLOT), BF16)

    qaug, slope, kpos, mine_k = [], [], [], []
    for x, (q_ref, s_ref) in enumerate(((qa_ref, sa_ref), (qb_ref, sb_ref))):
        h = hd_ref[pair, x]
        base = 0 if x else HEAD_DIM
        coef = jnp.zeros((tq, SLOT), F32)
        for j in range(N_COEF):
            coef = jnp.where(lane == base + j, sl_ref[h, j], coef)
        mine = (lane >= HEAD_DIM) if x else lower
        qx = jnp.where(mine, q_ref[0].astype(F32), coef).astype(BF16)
        qaug.append(jnp.concatenate([qx, s_ref[0]], axis=1))
        slope.append(sl_ref[h, N_COEF])
        in_coef = (lane_k >= base) & (lane_k < base + N_COEF)
        kpos.append(jnp.where(in_coef, row_kf, 0.0).astype(BF16))
        mine_k.append((lane_k >= HEAD_DIM) if x else klower)
        acc_ref[x] = jnp.zeros((tq, SLOT), F32)

    def scores(n):
        off = pl.multiple_of(n * KV_TILE, KV_TILE)
        k = k_ref[0, pl.ds(off, KV_TILE), :]
        block = n * (KV_TILE // SLC_BLK) + blk_of_row if nsa else n
        onehot = (lane_s == block).astype(BF16)
        return tuple(_dot_nt(qaug[x], jnp.concatenate([jnp.where(mine_k[x], k, kpos[x]), onehot], axis=1))
                     for x in range(2))

    def consume(n, us, ms, masked):
        v = v_ref[0, pl.ds(pl.multiple_of(n * KV_TILE, KV_TILE), KV_TILE), :]
        tile_off = ((n - i) * KV_TILE).astype(F32)
        out = []
        for x in range(2):
            u = jnp.where(causal, us[x], NEG_BIG) if masked else us[x]
            beta = slope[x] * tile_off
            m_new = jnp.maximum(ms[x], jnp.max(u, axis=1, keepdims=True) + beta)
            alpha = jnp.exp(ms[x] - m_new)
            p = jnp.exp(u - (m_new - beta))
            acc_ref[x] = alpha * acc_ref[x] + _dot(p.astype(BF16), jnp.where(mine_k[x], v, ones))
            out.append(m_new)
        return tuple(out)

    def body(n, carry):
        ms, us = carry
        us_next = scores(n + 1)
        return consume(n, us, ms, False), us_next

    m0 = jnp.full((tq, 1), M_INIT, F32)
    ms, us = lax.fori_loop(0, i, body, ((m0, m0), scores(jnp.int32(0))))
    consume(i, us, ms, True)

    acc_a = acc_ref[0]
    acc_b = acc_ref[1]
    o_a = acc_a / acc_a[:, HEAD_DIM:HEAD_DIM + 1]
    o_b = acc_b / acc_b[:, 0:1]
    if nsa:
        oa_ref[0] = (pa_ref[0] + ga_ref[0] * o_a).astype(BF16)
        ob_ref[0] = (pb_ref[0] + gb_ref[0] * o_b).astype(BF16)
    else:
        o_ref[0] = jnp.where(lower, o_a, o_b).astype(BF16)


def _flash_moba(q, selb, kvb, heads, sl_tab):
    b, t, wa = q.shape
    npair = wa // SLOT
    tq = KV_TILE
    slot = lambda f: pl.BlockSpec((1, tq, SLOT), lambda bi, p, i, hd, sl: (bi, i, f(p)))
    kv = lambda f: pl.BlockSpec((1, t, SLOT), lambda bi, p, i, hd, sl: (bi, 0, f(p)))
    grid_spec = pltpu.PrefetchScalarGridSpec(
        num_scalar_prefetch=2,
        grid=(b, npair, t // tq),
        in_specs=[slot(lambda p: p), slot(lambda p: p), slot(lambda p: 2 * p), slot(lambda p: 2 * p + 1),
                  kv(lambda p: p), kv(lambda p: npair + p)],
        out_specs=slot(lambda p: p),
        scratch_shapes=[pltpu.VMEM((2, tq, SLOT), F32)],
    )
    return pl.pallas_call(
        functools.partial(_flash_body, tq=tq, nsa=False),
        grid_spec=grid_spec,
        out_shape=jax.ShapeDtypeStruct((b, t, wa), BF16),
        compiler_params=_params(("parallel", "parallel", "arbitrary")),
        name="flash_moba",
    )(heads, sl_tab, q, q, selb, selb, kvb, kvb)


def _flash_nsa(qcx, selb, slcb, part, g1x, heads, sl_tab):
    b, t, wq = qcx.shape
    npair = wq // SLOT // 2
    nsp = selb.shape[2] // 2
    tq = KV_TILE
    slot = lambda f: pl.BlockSpec((1, tq, SLOT), lambda bi, p, i, hd, sl: (bi, i, f(p)))
    sel = lambda g: pl.BlockSpec((1, tq, nsp), lambda bi, p, i, hd, sl: (bi, i, g))
    kv = lambda c: pl.BlockSpec((1, t, SLOT), lambda bi, p, i, hd, sl: (bi, 0, c))
    lo, hi = (lambda p: p), (lambda p: npair + p)
    grid_spec = pltpu.PrefetchScalarGridSpec(
        num_scalar_prefetch=2,
        grid=(b, npair, t // tq),
        in_specs=[slot(lo), slot(hi), sel(0), sel(1), kv(0), kv(1), slot(lo), slot(hi), slot(lo), slot(hi)],
        out_specs=[slot(lo), slot(lo)],
        scratch_shapes=[pltpu.VMEM((2, tq, SLOT), F32)],
    )
    return pl.pallas_call(
        functools.partial(_flash_body, tq=tq, nsa=True),
        grid_spec=grid_spec,
        out_shape=[jax.ShapeDtypeStruct((b, t, npair * SLOT), BF16)] * 2,
        compiler_params=_params(("parallel", "parallel", "arbitrary")),
        name="flash_nsa",
    )(heads, sl_tab, qcx, qcx, selb, selb, slcb, slcb, part, part, g1x, g1x)


def _page_tile(p0_ref, p1_ref, tail_ref, is_past):
    w = tail_ref.shape[1]
    pages = jnp.concatenate([p0_ref[...].reshape(w, PAGE_SIZE), p1_ref[...].reshape(w, PAGE_SIZE)], axis=1)
    return jnp.where(is_past, pages, tail_ref[0]).astype(BF16)


def _decode_body(pt_ref, q_ref, s_ref, slope_ref, k0_ref, k1_ref, v0_ref, v1_ref, kt_ref, vt_ref, *rest,
                 n_last, nsa):
    if nsa:
        part_ref, g1_ref, o_ref, m_ref, l_ref, acc_ref = rest
    else:
        o_ref, m_ref, l_ref, acc_ref = rest
    n = pl.program_id(1)
    rows = q_ref.shape[1]
    selw = s_ref.shape[2]

    @pl.when(n == 0)
    def _():
        m_ref[...] = jnp.full(m_ref.shape, M_INIT, F32)
        l_ref[...] = jnp.zeros(l_ref.shape, F32)
        acc_ref[...] = jnp.zeros(acc_ref.shape, F32)

    is_past = n < n_last
    kt = _page_tile(k0_ref, k1_ref, kt_ref, is_past)
    vt = _page_tile(v0_ref, v1_ref, vt_ref, is_past)
    sel_row = lax.broadcasted_iota(jnp.int32, (selw, KV_TILE), 0)
    key = lax.broadcasted_iota(jnp.int32, (selw, KV_TILE), 1)
    block = n * (KV_TILE // SLC_BLK) + _div_pow2(key, SLC_BLK) if nsa else n
    ext = jnp.where(sel_row >= selw - N_COEF, key.astype(F32), (sel_row == block).astype(F32)).astype(BF16)
    qaug = jnp.concatenate([q_ref[0], s_ref[0]], axis=1)
    u = _dot(qaug, jnp.concatenate([kt, ext], axis=0))
    token = lax.broadcasted_iota(jnp.int32, (rows, KV_TILE), 0) & (DEC_ROWS - 1)
    visible = is_past | (lax.broadcasted_iota(jnp.int32, (rows, KV_TILE), 1) <= token)
    u = jnp.where(visible, u, NEG_BIG)
    beta = slope_ref[:, 0:1] * ((n - n_last) * KV_TILE).astype(F32)
    m = m_ref[...]
    m_new = jnp.maximum(m, jnp.max(u, axis=1, keepdims=True) + beta)
    alpha = jnp.exp(m - m_new)
    p = jnp.exp(u - (m_new - beta))
    m_ref[...] = m_new
    l_ref[...] = alpha * l_ref[...] + jnp.sum(p, axis=1, keepdims=True)
    acc_ref[...] = alpha * acc_ref[...] + _dot_nt(p.astype(BF16), vt)

    @pl.when(n == n_last)
    def _():
        o = acc_ref[...] / l_ref[...]
        if nsa:
            o = part_ref[0] + g1_ref[0] * o
        o_ref[0] = o


def _decode(q_rows, s_rows, slope_rows, pages, k_idx, v_idx, tails, page_table, extra, name):
    b, rows, w = q_rows.shape
    selw = s_rows.shape[2]
    n_pages = page_table.shape[1]
    n_last = n_pages * PAGE_SIZE // KV_TILE
    heads = pages.shape[2]
    per_b = lambda width: pl.BlockSpec((1, rows, width), lambda bi, n, pt: (bi, 0, 0))

    def page(comp, k):
        return pl.BlockSpec((1, 1, heads, HEAD_DIM, PAGE_SIZE),
                            lambda bi, n, pt: (pt[bi, jnp.minimum(2 * n + k, n_pages - 1)], comp, 0, 0, 0))

    tail = lambda comp: pl.BlockSpec((1, w, KV_TILE), lambda bi, n, pt: (bi, comp, 0))
    grid_spec = pltpu.PrefetchScalarGridSpec(
        num_scalar_prefetch=1,
        grid=(b, n_last + 1),
        in_specs=[per_b(w), per_b(selw), pl.BlockSpec(slope_rows.shape, lambda bi, n, pt: (0, 0)),
                  page(k_idx, 0), page(k_idx, 1), page(v_idx, 0), page(v_idx, 1), tail(k_idx), tail(v_idx)]
        + [per_b(w) for _ in extra],
        out_specs=per_b(w),
        scratch_shapes=[pltpu.VMEM((rows, 1), F32), pltpu.VMEM((rows, 1), F32), pltpu.VMEM((rows, w), F32)],
    )
    return pl.pallas_call(
        functools.partial(_decode_body, n_last=n_last, nsa=bool(extra)),
        grid_spec=grid_spec,
        out_shape=jax.ShapeDtypeStruct((b, rows, w), F32),
        compiler_params=_params(("parallel", "arbitrary")),
        name=name,
    )(page_table, q_rows, s_rows, slope_rows, pages, pages, pages, pages, tails, tails, *extra)


def _paged_kmean_body(pt_ref, k0_ref, k1_ref, km_ref):
    n = pl.program_id(1)
    w = km_ref.shape[1]

    @pl.when(n == 0)
    def _():
        km_ref[...] = jnp.zeros(km_ref.shape, F32)

    tot = (jnp.sum(k0_ref[...].reshape(w, PAGE_SIZE), axis=1, keepdims=True)
           + jnp.sum(k1_ref[...].reshape(w, PAGE_SIZE), axis=1, keepdims=True)) * (1.0 / MOBA_BLOCK)
    lane = lax.broadcasted_iota(jnp.int32, (w, SLOT), 1)
    km_ref[0] = jnp.where(lane == n, tot, km_ref[0])


def _paged_kmean(pages, page_table):
    b, n_pages = page_table.shape
    heads = pages.shape[2]
    w = heads * HEAD_DIM
    n_past = n_pages * PAGE_SIZE // MOBA_BLOCK
    assert n_past <= SLOT

    def page(k):
        return pl.BlockSpec((1, 1, heads, HEAD_DIM, PAGE_SIZE), lambda bi, n, pt: (pt[bi, 2 * n + k], 0, 0, 0, 0))

    grid_spec = pltpu.PrefetchScalarGridSpec(
        num_scalar_prefetch=1,
        grid=(b, n_past),
        in_specs=[page(0), page(1)],
        out_specs=pl.BlockSpec((1, w, SLOT), lambda bi, n, pt: (bi, 0, 0)),
    )
    return pl.pallas_call(
        _paged_kmean_body,
        grid_spec=grid_spec,
        out_shape=jax.ShapeDtypeStruct((b, w, SLOT), F32),
        compiler_params=_params(("parallel", "arbitrary")),
        name="paged_kmean",
    )(page_table, pages, pages)


def _cmp1_body(rawk_ref, rawv_ref, pos_ref, w_ref, u_ref, *, rows):
    nc = rows // CMP_STRIDE
    acc = [jnp.zeros((nc, SLOT), F32) for _ in range(4)]
    for r in range(CMP_STRIDE):
        for kv, raw_ref in enumerate((rawk_ref, rawv_ref)):
            xs = raw_ref[0, pl.ds(r, nc, stride=CMP_STRIDE), :]
            for half in range(2):
                rr = half * CMP_STRIDE + r
                xp = (xs + pos_ref[rr:rr + 1, kv * SLOT:(kv + 1) * SLOT]).astype(BF16)
                acc[2 * kv + half] = acc[2 * kv + half] + _dot(xp, w_ref[kv, half, r])
    u_ref[0] = jnp.concatenate(acc, axis=1)


def _cmp1(raw, pos_e, w1e, rows):
    b, l_pad, _ = raw.shape
    nc = rows // CMP_STRIDE
    return pl.pallas_call(
        functools.partial(_cmp1_body, rows=rows),
        grid=(b, l_pad // rows),
        in_specs=[
            pl.BlockSpec((1, rows, SLOT), lambda bi, i: (bi, i, 0)),
            pl.BlockSpec((1, rows, SLOT), lambda bi, i: (bi, i, 1)),
            pl.BlockSpec(pos_e.shape, lambda bi, i: (0, 0)),
            pl.BlockSpec(w1e.shape, lambda bi, i: (0, 0, 0, 0, 0)),
        ],
        out_specs=pl.BlockSpec((1, nc, 4 * SLOT), lambda bi, i: (bi, i, 0)),
        out_shape=jax.ShapeDtypeStruct((b, l_pad // CMP_STRIDE, 4 * SLOT), F32),
        compiler_params=_params(("parallel", "parallel")),
        name="cmp1",
    )(raw, raw, pos_e, w1e)


def _cmp2_body(ua_ref, ub_ref, w2_ref, seg_ref, gain_ref, kc_ref, vc_ref):
    ua = ua_ref[0]
    ub = ub_ref[0]
    pre_k = ua[:, 0:SLOT] + ub[:, SLOT:2 * SLOT]
    pre_v = ua[:, 2 * SLOT:3 * SLOT] + ub[:, 3 * SLOT:4 * SLOT]
    ck = _dot(jax.nn.gelu(pre_k).astype(BF16), w2_ref[0])
    cv = _dot(jax.nn.gelu(pre_v).astype(BF16), w2_ref[1])
    kc_ref[0] = _head_rms(ck, seg_ref[...], gain_ref[...]).astype(BF16)
    vc_ref[0] = cv.astype(BF16)


def _cmp2(ua, ub, w2e, seg, gain3):
    b, ncp, _ = ua.shape
    blk = pl.BlockSpec((1, ncp, 4 * SLOT), lambda bi: (bi, 0, 0))
    out = pl.BlockSpec((1, ncp, SLOT), lambda bi: (bi, 0, 0))
    return pl.pallas_call(
        _cmp2_body,
        grid=(b,),
        in_specs=[blk, blk, pl.BlockSpec(w2e.shape, lambda bi: (0, 0, 0)),
                  pl.BlockSpec(seg.shape, lambda bi: (0, 0)), pl.BlockSpec(gain3.shape, lambda bi: (0, 0))],
        out_specs=[out, out],
        out_shape=[jax.ShapeDtypeStruct((b, ncp, SLOT), BF16)] * 2,
        compiler_params=_params(("parallel",)),
        name="cmp2",
    )(ua, ub, w2e, seg, gain3)


def _nsa_sel_body(sl_ref, q_ref, g_ref, kc_ref, vc_ref, ov_ref, wk_ref, wv_ref, part_ref, g1_ref, selb_ref, *,
                  pos0, tq, n_cmp, pw0, lw_valid):
    t0 = pos0 + pl.program_id(1) * tq
    q = q_ref[0]
    gs = g_ref[0]
    ncp = kc_ref.shape[1]
    nsp = ov_ref.shape[1]
    n_wtiles = wk_ref.shape[1] // KV_TILE
    t = t0 + lax.broadcasted_iota(jnp.int32, (tq, 1), 0)
    t3 = jnp.concatenate([t, t, t], axis=0)
    lane = lax.broadcasted_iota(jnp.int32, (tq, SLOT), 1)
    mcol = lax.broadcasted_iota(jnp.int32, (tq, nsp), 1)
    mcol_f = mcol.astype(F32)
    col = lax.broadcasted_iota(jnp.int32, (1, KV_TILE), 1)
    own = _div_pow2(t, SLC_BLK)

    parts, g1s, selbs = [], [], []
    for g in range(2):
        q3 = jnp.concatenate([q[:, (3 * g + r) * SLOT:(3 * g + r + 1) * SLOT] for r in range(3)], axis=0)
        slope3 = jnp.concatenate([jnp.full((tq, 1), sl_ref[3 * g + r], F32) for r in range(3)], axis=0)

        cend = lax.broadcasted_iota(jnp.int32, (1, ncp), 1) * CMP_STRIDE + (CMP_BLK - 1)
        dist = t3 - cend
        s = _dot_nt(q3, kc_ref[0]) - slope3 * dist.astype(F32)
        mask = (dist >= 0) & (cend < n_cmp * CMP_STRIDE + (CMP_BLK - 1))
        p = _masked_softmax_rows(s, mask)
        o_cmp = _dot(p.astype(BF16), vc_ref[0])
        p_hi, p_lo = _split_bf16(p[0:tq] + p[tq:2 * tq] + p[2 * tq:3 * tq])
        imp = _dot(p_hi, ov_ref[...]) + _dot(p_lo, ov_ref[...])

        forced = (mcol == 0) | (mcol == own) | (mcol == own - 1)
        score = jnp.where(forced, FORCE_SCORE, jnp.where(mcol <= own, imp, -jnp.inf))
        sel = _topk_select(score, mcol_f, SLC_TOPK, jnp.zeros((tq, nsp), jnp.bool_))
        selbs.append(jnp.where(sel, 0.0, NEG_BIG).astype(BF16))

        wb = (t0 - pw0) // KV_TILE - 2
        s_parts, v_parts, m_parts = [], [], []
        for j in range(3):
            tile = wb + j
            off = pl.multiple_of(jnp.clip(tile, 0, n_wtiles - 1) * KV_TILE, KV_TILE)
            ridx = tile * KV_TILE + col
            dist = t3 - (pw0 + ridx)
            s_parts.append(_dot_nt(q3, wk_ref[0, pl.ds(off, KV_TILE), :]) - slope3 * dist.astype(F32))
            m_parts.append((ridx >= 0) & (ridx < lw_valid) & (dist >= 0) & (dist <= WINDOW))
            v_parts.append(wv_ref[0, pl.ds(off, KV_TILE), :])
        p = _masked_softmax_rows(jnp.concatenate(s_parts, axis=1), jnp.concatenate(m_parts, axis=1))
        o_win = sum(_dot(p[:, j * KV_TILE:(j + 1) * KV_TILE].astype(BF16), v_parts[j]) for j in range(3))

        in_group = (lane >= HEAD_DIM) if g else (lane < HEAD_DIM)
        for r in range(3):
            hd = 3 * g + r
            rows = slice(r * tq, (r + 1) * tq)
            o = gs[:, 3 * hd:3 * hd + 1] * o_cmp[rows] + gs[:, 3 * hd + 2:3 * hd + 3] * o_win[rows]
            parts.append(jnp.where(in_group, o, 0.0))
            g1s.append(jnp.where(in_group, gs[:, 3 * hd + 1:3 * hd + 2], 0.0))
    part_ref[0] = jnp.concatenate(parts, axis=1)
    g1_ref[0] = jnp.concatenate(g1s, axis=1)
    selb_ref[0] = jnp.concatenate(selbs, axis=1)


def _nsa_sel(qcx, gsig, kc, vc, ov, winb, slopes, pos0, tq, n_cmp, pw0, lw_valid):
    b, nq, wq = qcx.shape
    lw = winb.shape[1]
    ncp = kc.shape[1]
    nsp = ov.shape[1]
    grid_spec = pltpu.PrefetchScalarGridSpec(
        num_scalar_prefetch=1,
        grid=(b, nq // tq),
        in_specs=[
            pl.BlockSpec((1, tq, wq), lambda bi, i, sl: (bi, i, 0)),
            pl.BlockSpec((1, tq, SLOT), lambda bi, i, sl: (bi, i, 0)),
            pl.BlockSpec((1, ncp, SLOT), lambda bi, i, sl: (bi, 0, 0)),
            pl.BlockSpec((1, ncp, SLOT), lambda bi, i, sl: (bi, 0, 0)),
            pl.BlockSpec(ov.shape, lambda bi, i, sl: (0, 0)),
            pl.BlockSpec((1, lw, SLOT), lambda bi, i, sl: (bi, 0, 0)),
            pl.BlockSpec((1, lw, SLOT), lambda bi, i, sl: (bi, 0, 1)),
        ],
        out_specs=[pl.BlockSpec((1, tq, wq), lambda bi, i, sl: (bi, i, 0)),
                   pl.BlockSpec((1, tq, wq), lambda bi, i, sl: (bi, i, 0)),
                   pl.BlockSpec((1, tq, 2 * nsp), lambda bi, i, sl: (bi, i, 0))],
    )
    return pl.pallas_call(
        functools.partial(_nsa_sel_body, pos0=pos0, tq=tq, n_cmp=n_cmp, pw0=pw0, lw_valid=lw_valid),
        grid_spec=grid_spec,
        out_shape=[jax.ShapeDtypeStruct((b, nq, wq), F32), jax.ShapeDtypeStruct((b, nq, wq), F32),
                   jax.ShapeDtypeStruct((b, nq, 2 * nsp), BF16)],
        compiler_params=_params(("parallel", "arbitrary")),
        name="nsa_sel",
    )(slopes, qcx, gsig, kc, vc, ov, winb, winb)


def _conv_body(cin_ref, cw_ref, cb_ref, lg_ref, lb_ref, y_ref, *, tm):
    off = pl.multiple_of(pl.program_id(1) * tm, 8)
    win = cin_ref[0, pl.ds(off, tm + CONV_PAD), :]
    acc = jnp.zeros((tm, win.shape[1]), F32)
    for w in range(CONV_W):
        lo = w + CONV_PAD - (CONV_W - 1)
        acc = acc + cw_ref[w:w + 1, :] * win[lo:lo + tm, :]
    acc = acc + cb_ref[...]
    mu = jnp.mean(acc, axis=-1, keepdims=True)
    cen = acc - mu
    var = jnp.mean(cen * cen, axis=-1, keepdims=True)
    yn = cen * lax.rsqrt(var + EPS) * lg_ref[...] + lb_ref[...]
    y_ref[0] = (yn * _sigmoid(yn)).astype(BF16)


def _conv(cin, cw, cb, lg, lb, tm):
    b, rows, ch = cin.shape
    t_pad = rows - CONV_PAD
    full2 = lambda a: pl.BlockSpec(a.shape, lambda bi, i: (0, 0))
    return pl.pallas_call(
        functools.partial(_conv_body, tm=tm),
        grid=(b, t_pad // tm),
        in_specs=[pl.BlockSpec((1, rows, ch), lambda bi, i: (bi, 0, 0)), full2(cw), full2(cb), full2(lg), full2(lb)],
        out_specs=pl.BlockSpec((1, tm, ch), lambda bi, i: (bi, i, 0)),
        out_shape=jax.ShapeDtypeStruct((b, t_pad, ch), BF16),
        compiler_params=_params(("parallel", "arbitrary")),
        name="conv",
    )(cin, cw, cb, lg, lb)


def _ffn_body(x_ref, oa_ref, y_ref, oc0_ref, oc1_ref, wa_ref, wy_ref, wc0_ref, wc1_ref, g2_ref, wu_ref, wd_ref,
              out_ref):
    x1 = (x_ref[...] + _dot(oa_ref[...], wa_ref[...]) + _dot(y_ref[...], wy_ref[...])
          + _dot(oc0_ref[...], wc0_ref[...]) + _dot(oc1_ref[...], wc1_ref[...]))
    ms = jnp.mean(x1 * x1, axis=-1, keepdims=True)
    h2 = (x1 * lax.rsqrt(ms + EPS) * g2_ref[...]).astype(BF16)
    u = jnp.maximum(_dot(h2, wu_ref[...]), 0.0)
    out_ref[...] = x1 + _dot((u * u).astype(BF16), wd_ref[...])


def _ffn(x2d, acts, weights, tm):
    m, d = x2d.shape
    row = lambda a: pl.BlockSpec((tm, a.shape[1]), lambda i: (i, 0))
    const = lambda a: pl.BlockSpec(a.shape, lambda i: (0, 0), pipeline_mode=pl.Buffered(1))
    return pl.pallas_call(
        _ffn_body,
        grid=(m // tm,),
        in_specs=[row(x2d)] + [row(a) for a in acts] + [const(w) for w in weights],
        out_specs=pl.BlockSpec((tm, d), lambda i: (i, 0)),
        out_shape=jax.ShapeDtypeStruct((m, d), F32),
        compiler_params=_params(("parallel",)),
        name="out_ffn",
    )(x2d, *acts, *weights)


GATHER_PAGES = 4


def _gather_rows_body(pt_ref, *refs, n_past):
    *page_refs, tail_ref, out_ref = refs
    blk = jnp.concatenate([r[0] for r in page_refs], axis=0)
    out_ref[0] = jnp.where(pl.program_id(1) < n_past, blk, tail_ref[0])


def _gather_rows(cache, tail, page_table):
    bsz, n_pages = page_table.shape
    w = cache.shape[-1]
    assert n_pages % GATHER_PAGES == 0
    n_past = n_pages // GATHER_PAGES
    rows = GATHER_PAGES * PAGE_SIZE

    def page(k):
        return pl.BlockSpec((1, PAGE_SIZE, w),
                            lambda bi, n, pt: (pt[bi, jnp.minimum(GATHER_PAGES * n + k, n_pages - 1)], 0, 0))

    grid_spec = pltpu.PrefetchScalarGridSpec(
        num_scalar_prefetch=1,
        grid=(bsz, n_past + 1),
        in_specs=[page(k) for k in range(GATHER_PAGES)] + [pl.BlockSpec((1, rows, w), lambda bi, n, pt: (bi, 0, 0))],
        out_specs=pl.BlockSpec((1, rows, w), lambda bi, n, pt: (bi, n, 0)),
    )
    return pl.pallas_call(
        functools.partial(_gather_rows_body, n_past=n_past),
        grid_spec=grid_spec,
        out_shape=jax.ShapeDtypeStruct((bsz, (n_past + 1) * rows, w), F32),
        compiler_params=_params(("parallel", "arbitrary")),
        name="gather_rows",
    )(page_table, *([cache] * GATHER_PAGES), tail)


def _layer_consts(lp):
    (ln1_g, w_in, qk_gain, cmp_pos, cmp_w1, cmp_w2, conv_w, conv_b, conv_ln_g, conv_ln_b, w_out, ln2_g, w_up,
     w_down) = lp
    d = w_in.shape[0]
    z64 = jnp.zeros((d, HEAD_DIM), F32)
    qc = w_in[:, 1664:2048].reshape(d, 6, HEAD_DIM)
    qc_slots = [jnp.concatenate([qc[:, h], z64] if h < 3 else [z64, qc[:, h]], axis=1) for h in range(6)]
    gate = jnp.pad(w_in[:, 2816:2834], ((0, 0), (0, SLOT - 18)))
    w_re = jnp.concatenate([w_in[:, 0:1664]] + qc_slots + [w_in[:, 2048:2816], gate], axis=1).astype(BF16)

    ones = lambda n: jnp.ones((n,), F32)
    g = qk_gain
    gain_row = jnp.concatenate([
        jnp.tile(g[0], 6) * SCALE, jnp.tile(g[1], 6), ones(C_QC - C_VA),
        jnp.tile(g[2], 12) * SCALE, ones(2 * SLOT), jnp.tile(g[4], 2), ones(SLOT),
        jnp.tile(g[5], 2), ones(SLOT), ones(SLOT)])[None, :]
    seg = jnp.asarray(np.kron(np.eye(2, dtype=np.float32), np.full((HEAD_DIM, HEAD_DIM), 1.0 / HEAD_DIM, np.float32)),
                      BF16)

    eye2 = jnp.eye(2, dtype=F32)
    w1 = cmp_w1.reshape(2, 2, CMP_STRIDE, HEAD_DIM, HEAD_DIM)
    w1e = jnp.einsum("ab,khrdj->khradbj", eye2, w1).reshape(2, 2, CMP_STRIDE, SLOT, SLOT).astype(BF16)
    w2e = jnp.einsum("ab,kdj->kadbj", eye2, cmp_w2).reshape(2, SLOT, SLOT).astype(BF16)
    pos_e = jnp.concatenate([jnp.tile(cmp_pos[0], (1, 2)), jnp.tile(cmp_pos[1], (1, 2))], axis=1)
    gain3 = jnp.tile(g[3], 2)[None, :]

    wo = w_out
    wa = wo[0:384].astype(BF16)
    wy = wo[384:640].astype(BF16)
    wc_rows = wo[640:1024].reshape(6, HEAD_DIM, d)
    zr = jnp.zeros((HEAD_DIM, d), F32)
    wc = jnp.concatenate([jnp.concatenate([wc_rows[h], zr] if h < 3 else [zr, wc_rows[h]], axis=0)
                          for h in range(6)], axis=0).astype(BF16)
    ffn_w = (wa, wy, wc[:3 * SLOT], wc[3 * SLOT:], ln2_g[None, :], w_up.astype(BF16), w_down.astype(BF16))
    return dict(ln1=ln1_g[None, :], w_re=w_re, gain_row=gain_row, seg=seg, w1e=w1e, w2e=w2e, pos_e=pos_e,
                gain3=gain3, conv_w=conv_w, conv_b=conv_b[None, :], conv_lg=conv_ln_g[None, :],
                conv_lb=conv_ln_b[None, :], ffn_w=ffn_w)


def _overlap_matrix(ncp, nsp, n_cmp, n_slc):
    n = np.arange(ncp)[:, None]
    m = np.arange(nsp)[None, :]
    ov = ((m == (n * CMP_STRIDE) // SLC_BLK) | (m == (n * CMP_STRIDE + CMP_BLK - 1) // SLC_BLK))
    ov = ov & (n < n_cmp) & (m < n_slc)
    return jnp.asarray(ov.astype(np.float32), BF16)


def _round_up(x, m):
    return -(-x // m) * m


def _pick_rows(l_pad):
    for rows in (2048, 1280, 1024, 768, 512, 256):
        if l_pad % rows == 0:
            return rows
    raise ValueError(l_pad)


def _head_rows(a, t_rows):
    b = a.shape[0]
    w = a.shape[2] // N_HEADS
    return a.reshape(b, t_rows, N_HEADS, w).transpose(0, 2, 1, 3).reshape(b, N_HEADS * t_rows, w)


def _with_coefs(s_rows, sl_tab, n_blocks):
    selw = s_rows.shape[2]
    assert n_blocks <= selw - N_COEF
    coef = np.zeros((N_HEADS * DEC_ROWS, selw), np.float32)
    coef[:, selw - N_COEF:] = np.repeat(sl_tab[:, :N_COEF], DEC_ROWS, axis=0)
    lane = np.arange(selw)[None, :] >= selw - N_COEF
    return jnp.where(jnp.asarray(lane)[None], jnp.asarray(coef, BF16)[None], s_rows)


def _layer(x, c, past, tabs):
    b, t, d = x.shape
    m = b * t
    tm = 256 if m % 256 == 0 else m
    sl_np = tabs["sl_np"]
    qa, kva, kvab, glu, qcx, kvc, slcb, win, winb, gsig, kmean_t = _proj(
        x.reshape(m, d), c["ln1"], c["w_re"], c["seg"], c["gain_row"], tm)
    r3 = lambda a: a.reshape(b, t, a.shape[-1])
    qa, kva, kvab, glu, qcx, kvc, slcb, win, winb, gsig = map(r3, (qa, kva, kvab, glu, qcx, kvc, slcb, win, winb, gsig))
    decode = past is not None

    if not decode:
        past_len = 0
        tq = KV_TILE
        assert t % KV_TILE == 0
        l_pad = t
        nb = t // MOBA_BLOCK
        kmean = kmean_t[:, 0, :].reshape(b, nb, 384)
        raw = kvc
        win_all = winb
        lw_valid = t
        pw0 = 0
        q_a, q_c, g_c = qa, qcx, gsig
        new_win_state = win[:, -min(WINDOW, t):]
        conv_in = jnp.pad(glu, ((0, 0), (CONV_PAD, 0), (0, 0)))
        new_conv = glu[:, -(CONV_W - 1):]
        conv_tm = 256
    else:
        page_table = past["page_table"]
        past_len = page_table.shape[1] * PAGE_SIZE
        assert past_len % KV_TILE == 0 and t <= DEC_ROWS
        tq = DEC_ROWS
        l_pad = past_len + KV_TILE
        nb = l_pad // MOBA_BLOCK
        to_tail = lambda a: jnp.pad(a, ((0, 0), (0, KV_TILE - t), (0, 0))).transpose(0, 2, 1)
        tail_a, tail_c = to_tail(kva), to_tail(kvc)
        kmean = _paged_kmean(past["moba_t"], page_table).transpose(0, 2, 1)[:, :nb]
        raw_tail = jnp.pad(kvc[:, :, :2 * SLOT], ((0, 0), (0, GATHER_PAGES * PAGE_SIZE - t), (0, 0)))
        raw = _gather_rows(past["nsa_raw"], raw_tail, page_table)
        win_buf = past["win"]
        nbuf = win_buf.shape[1]
        lw_valid = nbuf + t
        win_cat = jnp.concatenate([win_buf, win], axis=1)
        win_all = jnp.pad(win_cat, ((0, 0), (0, _round_up(lw_valid, KV_TILE) - lw_valid), (0, 0))).astype(BF16)
        pw0 = past_len - nbuf
        padq = lambda a: jnp.pad(a, ((0, 0), (0, tq - t), (0, 0)))
        q_a, q_c, g_c = padq(qa), padq(qcx), padq(gsig)
        new_win_state = win_cat[:, -nbuf:]
        conv_cat = jnp.concatenate([past["conv"], glu], axis=1)
        conv_in = jnp.pad(conv_cat, ((0, 0), (CONV_PAD - (CONV_W - 1), 8 - t), (0, 0)))
        new_conv = conv_cat[:, -(CONV_W - 1):]
        conv_tm = 8

    n_cmp = l_pad // CMP_STRIDE - 1
    n_slc = l_pad // SLC_BLK
    ncp = _round_up(n_cmp, SLOT)
    nsp = _round_up(n_slc + (N_COEF if decode else 0), SLOT)
    u = _cmp1(raw, c["pos_e"], c["w1e"], _pick_rows(raw.shape[1]))
    ua = jnp.pad(u[:, :n_cmp], ((0, 0), (0, ncp - n_cmp), (0, 0)))
    ub = jnp.pad(u[:, 1:n_cmp + 1], ((0, 0), (0, ncp - n_cmp), (0, 0)))
    kc, vc = _cmp2(ua, ub, c["w2e"], c["seg"], c["gain3"])
    ov = _overlap_matrix(ncp, nsp, n_cmp, n_slc)
    part, g1x, selb_c = _nsa_sel(q_c, g_c, kc, vc, ov, win_all, tabs["slopes"], past_len, tq, n_cmp, pw0, lw_valid)

    kmean = jnp.pad(kmean, ((0, 0), (0, SLOT - nb), (0, 0)))
    selb_a = _moba_gate(q_a, kmean, past_len, tq)

    if not decode:
        o_a = _flash_moba(q_a, selb_a, kvab, tabs["heads_a"], tabs["sl_tab"])
        oc0, oc1 = _flash_nsa(q_c, selb_c, slcb, part, g1x, tabs["heads_c"], tabs["sl_tab"])
    else:
        slope_rows =jnp.asarray(np.repeat(sl_np[:, N_COEF:], DEC_ROWS, axis=0) * np.ones((1, SLOT), np.float32))
        head_of_lane = np.arange(N_HEADS * HEAD_DIM)[None, :] // HEAD_DIM == np.arange(N_HEADS)[:, None]
        q_rows = jnp.where(jnp.asarray(head_of_lane)[None, :, None, :], q_a[:, None], jnp.zeros((), BF16))
        q_rows = q_rows.reshape(b, N_HEADS * DEC_ROWS, q_a.shape[2])
        s_rows = _with_coefs(_head_rows(selb_a, DEC_ROWS), sl_np, nb)
        o_rows = _decode(q_rows, s_rows, slope_rows, past["moba_t"], 0, 1, tail_a, page_table, (), "decode_moba")
        o_a = jnp.concatenate([o_rows[:, h * DEC_ROWS:h * DEC_ROWS + t, h * HEAD_DIM:(h + 1) * HEAD_DIM]
                               for h in range(N_HEADS)], axis=2).astype(BF16)
        sel_g = selb_c.reshape(b, DEC_ROWS, 2, nsp)
        s_rows = jnp.concatenate([sel_g[:, :, h // 3] for h in range(N_HEADS)], axis=1)
        s_rows = _with_coefs(s_rows, sl_np, n_slc)
        o_rows = _decode(_head_rows(q_c, DEC_ROWS), s_rows, slope_rows, past["nsa_t"], 2, 3, tail_c, page_table,
                         (_head_rows(part, DEC_ROWS), _head_rows(g1x, DEC_ROWS)), "decode_nsa")
        o_c = o_rows.reshape(b, N_HEADS, DEC_ROWS, SLOT).transpose(0, 2, 1, 3).reshape(b, DEC_ROWS, N_HEADS * SLOT)
        o_c = o_c[:, :t].astype(BF16)
        oc0, oc1 = o_c[:, :, :3 * SLOT], o_c[:, :, 3 * SLOT:]

    y = _conv(conv_in, c["conv_w"], c["conv_b"], c["conv_lg"], c["conv_lb"], conv_tm)[:, :t]

    f2 = lambda a: a.reshape(m, a.shape[-1])
    x_new = _ffn(x.reshape(m, d), (f2(o_a), f2(y), f2(oc0), f2(oc1)), c["ffn_w"], tm)
    return x_new.reshape(b, t, d), kva, kvc, new_win_state, new_conv


def _tables():
    sl_np = _slope_table(N_HEADS)
    return dict(sl_np=sl_np, sl_tab=jnp.asarray(sl_np), slopes=jnp.asarray(sl_np[:, N_COEF]),
                heads_a=jnp.asarray(np.array([[0, 1], [2, 3], [4, 5]], np.int32)),
                heads_c=jnp.asarray(np.array([[0, 3], [1, 4], [2, 5]], np.int32)))


def kernel(x_prompt, x_sample, cache_moba_kv, cache_nsa_kv, cache_win_kv, cache_conv, page_table, ln1_g, w_in,
           qk_gain, cmp_pos, cmp_w1, cmp_w2, conv_w, conv_b, conv_ln_g, conv_ln_b, w_out, ln2_g, w_up, w_down):
    depth = w_in.shape[0]
    bp, tp, _ = x_prompt.shape
    bs, ts, _ = x_sample.shape
    n_pool = cache_moba_kv.shape[1]
    tabs = _tables()
    yp, ys = x_prompt, x_sample
    outs = [[] for _ in range(8)]
    for l in range(depth):
        c = _layer_consts((ln1_g[l], w_in[l], qk_gain[l], cmp_pos[l], cmp_w1[l], cmp_w2[l], conv_w[l], conv_b[l],
                           conv_ln_g[l], conv_ln_b[l], w_out[l], ln2_g[l], w_up[l], w_down[l]))
        yp, a, cc, w, cv = _layer(yp, c, None, tabs)
        outs[0].append(a.reshape(bp, tp, 2, 6, HEAD_DIM))
        outs[2].append(cc.reshape(bp, tp, 4, 2, HEAD_DIM))
        outs[4].append(w.reshape(bp, w.shape[1], 2, 2, HEAD_DIM))
        outs[6].append(cv)
        past = dict(page_table=page_table + l * n_pool,
                    moba_t=cache_moba_kv.transpose(0, 1, 3, 4, 5, 2).reshape(depth * n_pool, 2, 6, HEAD_DIM, PAGE_SIZE),
                    nsa_t=cache_nsa_kv.transpose(0, 1, 3, 4, 5, 2).reshape(depth * n_pool, 4, 2, HEAD_DIM, PAGE_SIZE),
                    nsa_raw=cache_nsa_kv[:, :, :, 0:2].reshape(depth * n_pool, PAGE_SIZE, 2 * SLOT),
                    win=cache_win_kv[l].reshape(bs, cache_win_kv.shape[2], 256),
                    conv=cache_conv[l])
        ys, a, cc, w, cv = _layer(ys, c, past, tabs)
        outs[1].append(a.reshape(bs, ts, 2, 6, HEAD_DIM))
        outs[3].append(cc.reshape(bs, ts, 4, 2, HEAD_DIM))
        outs[5].append(w.reshape(bs, w.shape[1], 2, 2, HEAD_DIM))
        outs[7].append(cv)
    return (yp, ys) + tuple(jnp.stack(o) for o in outs)
```

```python
import functools

import numpy as np
import jax
import jax.numpy as jnp
from jax import lax
from jax.experimental import pallas as pl
from jax.experimental.pallas import tpu as pltpu

F32 = jnp.float32
BF16 = jnp.bfloat16

HEAD_DIM = 64
SLOT = 2 * HEAD_DIM
N_HEADS = 6
PAGE_SIZE = 128
MOBA_BLOCK = 256
MOBA_TOPK = 3
CMP_STRIDE = 16
CMP_BLK = 32
SLC_BLK = 64
SLC_TOPK = 16
WINDOW = 512
CONV_W = 31
CONV_PAD = 32
EPS = 1e-6
FORCE_SCORE = 1e4
SCALE = HEAD_DIM ** -0.5
NEG_BIG = -(2.0 ** 100)
M_INIT = -(2.0 ** 99)
KV_TILE = 256
DEC_ROWS = 16
N_COEF = 3
VMEM_LIMIT = 56 * 1024 * 1024

C_QA, C_KA, C_VA, C_GLU, C_QC, C_KVC, C_WIN, C_GATE, C_END = 0, 384, 768, 1152, 1664, 2432, 2944, 3200, 3328


def _alibi_slopes(n):
    return np.array([2.0 ** (-8.0 * (i + 1) / n) for i in range(n)], dtype=np.float32)


def _slope_table(n):
    sl = _alibi_slopes(n)
    terms, rest = [], sl.copy()
    for _ in range(N_COEF):
        term = rest.astype(BF16).astype(np.float32)
        terms.append(term)
        rest = rest - term
    assert not rest.any()
    return np.stack(terms + [sl], axis=1)


def _dot(a, b):
    return jnp.dot(a, b, preferred_element_type=F32)


def _dot_nt(a, b):
    return lax.dot_general(a, b, (((1,), (1,)), ((), ())), preferred_element_type=F32)


def _split_bf16(x):
    hi = x.astype(BF16)
    lo = (x - hi.astype(F32)).astype(BF16)
    return hi, lo


def _div_pow2(x, d):
    return lax.shift_right_logical(x, jnp.int32(int(d).bit_length() - 1))


def _sigmoid(x):
    return 1.0 / (1.0 + jnp.exp(-x))


def _params(sem, vmem=VMEM_LIMIT):
    return pltpu.CompilerParams(dimension_semantics=sem, vmem_limit_bytes=vmem)


def _head_rms(zs, seg, gain):
    cols = []
    for c in range(zs.shape[1] // SLOT):
        zz = zs[:, c * SLOT:(c + 1) * SLOT]
        hi, lo = _split_bf16(zz * zz)
        cols.append(_dot(hi, seg) + _dot(lo, seg))
    ms = cols[0] if len(cols) == 1 else jnp.concatenate(cols, axis=1)
    return zs * lax.rsqrt(ms + EPS) * gain


def _proj_body(x_ref, ln_ref, w_ref, seg_ref, gain_ref,
               qa_ref, kva_ref, kvab_ref, glu_ref, qcx_ref, kvc_ref, slcb_ref, win_ref, winb_ref,
               gsig_ref, kmean_ref):
    x = x_ref[...]
    ms = jnp.mean(x * x, axis=-1, keepdims=True)
    h = (x * lax.rsqrt(ms + EPS) * ln_ref[...]).astype(BF16)
    z = _dot(h, w_ref[...])
    seg = seg_ref[...]

    def normed(lo, hi):
        return _head_rms(z[:, lo:hi], seg, gain_ref[:, lo:hi])

    qa_ref[...] = normed(C_QA, C_KA).astype(BF16)
    ka = normed(C_KA, C_VA)
    kva = jnp.concatenate([ka, z[:, C_VA:C_GLU]], axis=1)
    kva_ref[...] = kva
    kvab_ref[...] = kva.astype(BF16)
    kmean_ref[0] = jnp.broadcast_to(jnp.sum(ka, axis=0, keepdims=True) * (1.0 / ka.shape[0]), (8, ka.shape[1]))
    half = (C_QC - C_GLU) // 2
    glu_ref[...] = z[:, C_GLU:C_GLU + half] * _sigmoid(z[:, C_GLU + half:C_QC])
    qcx_ref[...] = normed(C_QC, C_KVC).astype(BF16)
    slck = normed(C_KVC + 2 * SLOT, C_KVC + 3 * SLOT)
    kvc = jnp.concatenate([z[:, C_KVC:C_KVC + 2 * SLOT], slck, z[:, C_KVC + 3 * SLOT:C_WIN]], axis=1)
    kvc_ref[...] = kvc
    slcb_ref[...] = kvc[:, 2 * SLOT:].astype(BF16)
    wink = normed(C_WIN, C_WIN + SLOT)
    win = jnp.concatenate([wink, z[:, C_WIN + SLOT:C_GATE]], axis=1)
    win_ref[...] = win
    winb_ref[...] = win.astype(BF16)
    gsig_ref[...] = _sigmoid(z[:, C_GATE:C_END])


def _proj(x2d, ln_g, w_re, seg, gain_row, tm):
    m, d = x2d.shape
    nt = m // tm
    row = lambda w: pl.BlockSpec((tm, w), lambda i: (i, 0))
    full = lambda a: pl.BlockSpec(a.shape, lambda i: (0,) * a.ndim)
    outs = [(384, BF16), (768, F32), (768, BF16), (256, F32), (768, BF16), (512, F32), (256, BF16),
            (256, F32), (256, BF16), (128, F32)]
    return pl.pallas_call(
        _proj_body,
        grid=(nt,),
        in_specs=[row(d), full(ln_g), full(w_re), full(seg), full(gain_row)],
        out_specs=[row(w) for w, _ in outs] + [pl.BlockSpec((1, 8, 384), lambda i: (i, 0, 0))],
        out_shape=[jax.ShapeDtypeStruct((m, w), dt) for w, dt in outs]
        + [jax.ShapeDtypeStruct((nt, 8, 384), F32)],
        compiler_params=_params(("parallel",)),
        name="proj",
    )(x2d, ln_g, w_re, seg, gain_row)


def _topk_select(score, lane_f, k, sel):
    for _ in range(k):
        mx = jnp.max(score, axis=1, keepdims=True)
        idx = jnp.min(jnp.where(score == mx, lane_f, 1e9), axis=1, keepdims=True)
        hit = lane_f == idx
        sel = sel | (hit & (mx > -jnp.inf))
        score = jnp.where(hit, -jnp.inf, score)
    return sel


def _masked_softmax_rows(s, mask):
    sm = jnp.where(mask, s, NEG_BIG)
    m = jnp.maximum(jnp.max(sm, axis=1, keepdims=True), M_INIT)
    e = jnp.where(mask, jnp.exp(sm - m), 0.0)
    l = jnp.sum(e, axis=1, keepdims=True)
    return e / jnp.where(l > 0.0, l, 1.0)


def _moba_gate_body(q_ref, km_ref, s_ref, *, pos0, tq):
    t0 = pos0 + pl.program_id(1) * tq
    lane = lax.broadcasted_iota(jnp.int32, (tq, SLOT), 1)
    lane_f = lane.astype(F32)
    own = _div_pow2(t0 + lax.broadcasted_iota(jnp.int32, (tq, 1), 0), MOBA_BLOCK)
    for h in range(N_HEADS):
        cols = slice((h // 2) * SLOT, (h // 2 + 1) * SLOT)
        q = q_ref[0, :, cols]
        mine = (lane >= HEAD_DIM) if h % 2 else (lane < HEAD_DIM)
        qh = jnp.where(mine, q, jnp.zeros_like(q))
        km_hi, km_lo = _split_bf16(km_ref[0, :, cols])
        gate = _dot_nt(qh, km_hi) + _dot_nt(qh, km_lo)
        gate = jnp.where(lane < own, gate, -jnp.inf)
        sel = _topk_select(gate, lane_f, MOBA_TOPK, lane == own)
        s_ref[0, :, h * SLOT:(h + 1) * SLOT] = jnp.where(sel, 0.0, NEG_BIG).astype(BF16)


def _moba_gate(q, kmean, pos0, tq):
    b, nq, wa = q.shape
    return pl.pallas_call(
        functools.partial(_moba_gate_body, pos0=pos0, tq=tq),
        grid=(b, nq // tq),
        in_specs=[pl.BlockSpec((1, tq, wa), lambda bi, i: (bi, i, 0)),
                  pl.BlockSpec((1, SLOT, wa), lambda bi, i: (bi, 0, 0))],
        out_specs=pl.BlockSpec((1, tq, N_HEADS * SLOT), lambda bi, i: (bi, i, 0)),
        out_shape=jax.ShapeDtypeStruct((b, nq, N_HEADS * SLOT), BF16),
        compiler_params=_params(("parallel", "arbitrary")),
        name="moba_gate",
    )(q, kmean)


def _flash_body(hd_ref, sl_ref, qa_ref, qb_ref, sa_ref, sb_ref, k_ref, v_ref, *rest, tq, nsa):
    if nsa:
        pa_ref, pb_ref, ga_ref, gb_ref, oa_ref, ob_ref, acc_ref = rest
    else:
        o_ref, acc_ref = rest
    pair = pl.program_id(1)
    i = pl.program_id(2)
    selw = sa_ref.shape[2]
    lane = lax.broadcasted_iota(jnp.int32, (tq, SLOT), 1)
    lower = lane < HEAD_DIM
    lane_k = lax.broadcasted_iota(jnp.int32, (KV_TILE, SLOT), 1)
    klower = lane_k < HEAD_DIM
    row_kf = lax.broadcasted_iota(jnp.int32, (KV_TILE, SLOT), 0).astype(F32)
    lane_s = lax.broadcasted_iota(jnp.int32, (KV_TILE, selw), 1)
    blk_of_row = _div_pow2(lax.broadcasted_iota(jnp.int32, (KV_TILE, selw), 0), SLC_BLK)
    causal = lax.broadcasted_iota(jnp.int32, (tq, KV_TILE), 1) <= lax.broadcasted_iota(jnp.int32, (tq, KV_TILE), 0)
    ones = jnp.ones((KV_TILE, SLOT), BF16)

    qaug, slope, kpos, mine_k = [], [], [], []
    for x, (q_ref, s_ref) in enumerate(((qa_ref, sa_ref), (qb_ref, sb_ref))):
        h = hd_ref[pair, x]
        base = 0 if x else HEAD_DIM
        coef = jnp.zeros((tq, SLOT), F32)
        for j in range(N_COEF):
            coef = jnp.where(lane == base + j, sl_ref[h, j], coef)
        mine = (lane >= HEAD_DIM) if x else lower
        qx = jnp.where(mine, q_ref[0].astype(F32), coef).astype(BF16)
        qaug.append(jnp.concatenate([qx, s_ref[0]], axis=1))
        slope.append(sl_ref[h, N_COEF])
        in_coef = (lane_k >= base) & (lane_k < base + N_COEF)
        kpos.append(jnp.where(in_coef, row_kf, 0.0).astype(BF16))
        mine_k.append((lane_k >= HEAD_DIM) if x else klower)
        acc_ref[x] = jnp.zeros((tq, SLOT), F32)

    def scores(n):
        off = pl.multiple_of(n * KV_TILE, KV_TILE)
        k = k_ref[0, pl.ds(off, KV_TILE), :]
        block = n * (KV_TILE // SLC_BLK) + blk_of_row if nsa else n
        onehot = (lane_s == block).astype(BF16)
        return tuple(_dot_nt(qaug[x], jnp.concatenate([jnp.where(mine_k[x], k, kpos[x]), onehot], axis=1))
                     for x in range(2))

    def consume(n, us, ms, masked):
        v = v_ref[0, pl.ds(pl.multiple_of(n * KV_TILE, KV_TILE), KV_TILE), :]
        tile_off = ((n - i) * KV_TILE).astype(F32)
        out = []
        for x in range(2):
            u = jnp.where(causal, us[x], NEG_BIG) if masked else us[x]
            beta = slope[x] * tile_off
            m_new = jnp.maximum(ms[x], jnp.max(u, axis=1, keepdims=True) + beta)
            alpha = jnp.exp(ms[x] - m_new)
            p = jnp.exp(u - (m_new - beta))
            acc_ref[x] = alpha * acc_ref[x] + _dot(p.astype(BF16), jnp.where(mine_k[x], v, ones))
            out.append(m_new)
        return tuple(out)

    def body(n, carry):
        ms, us = carry
        us_next = scores(n + 1)
        return consume(n, us, ms, False), us_next

    m0 = jnp.full((tq, 1), M_INIT, F32)
    ms, us = lax.fori_loop(0, i, body, ((m0, m0), scores(jnp.int32(0))))
    consume(i, us, ms, True)

    acc_a = acc_ref[0]
    acc_b = acc_ref[1]
    o_a = acc_a / acc_a[:, HEAD_DIM:HEAD_DIM + 1]
    o_b = acc_b / acc_b[:, 0:1]
    if nsa:
        oa_ref[0] = (pa_ref[0] + ga_ref[0] * o_a).astype(BF16)
        ob_ref[0] = (pb_ref[0] + gb_ref[0] * o_b).astype(BF16)
    else:
        o_ref[0] = jnp.where(lower, o_a, o_b).astype(BF16)


def _flash_moba(q, selb, kvb, heads, sl_tab):
    b, t, wa = q.shape
    npair = wa // SLOT
    tq = KV_TILE
    slot = lambda f: pl.BlockSpec((1, tq, SLOT), lambda bi, p, i, hd, sl: (bi, i, f(p)))
    kv = lambda f: pl.BlockSpec((1, t, SLOT), lambda bi, p, i, hd, sl: (bi, 0, f(p)))
    grid_spec = pltpu.PrefetchScalarGridSpec(
        num_scalar_prefetch=2,
        grid=(b, npair, t // tq),
        in_specs=[slot(lambda p: p), slot(lambda p: p), slot(lambda p: 2 * p), slot(lambda p: 2 * p + 1),
                  kv(lambda p: p), kv(lambda p: npair + p)],
        out_specs=slot(lambda p: p),
        scratch_shapes=[pltpu.VMEM((2, tq, SLOT), F32)],
    )
    return pl.pallas_call(
        functools.partial(_flash_body, tq=tq, nsa=False),
        grid_spec=grid_spec,
        out_shape=jax.ShapeDtypeStruct((b, t, wa), BF16),
        compiler_params=_params(("parallel", "parallel", "arbitrary")),
        name="flash_moba",
    )(heads, sl_tab, q, q, selb, selb, kvb, kvb)


def _flash_nsa(qcx, selb, slcb, part, g1x, heads, sl_tab):
    b, t, wq = qcx.shape
    npair = wq // SLOT // 2
    nsp = selb.shape[2] // 2
    tq = KV_TILE
    slot = lambda f: pl.BlockSpec((1, tq, SLOT), lambda bi, p, i, hd, sl: (bi, i, f(p)))
    sel = lambda g: pl.BlockSpec((1, tq, nsp), lambda bi, p, i, hd, sl: (bi, i, g))
    kv = lambda c: pl.BlockSpec((1, t, SLOT), lambda bi, p, i, hd, sl: (bi, 0, c))
    lo, hi = (lambda p: p), (lambda p: npair + p)
    grid_spec = pltpu.PrefetchScalarGridSpec(
        num_scalar_prefetch=2,
        grid=(b, npair, t // tq),
        in_specs=[slot(lo), slot(hi), sel(0), sel(1), kv(0), kv(1), slot(lo), slot(hi), slot(lo), slot(hi)],
        out_specs=[slot(lo), slot(lo)],
        scratch_shapes=[pltpu.VMEM((2, tq, SLOT), F32)],
    )
    return pl.pallas_call(
        functools.partial(_flash_body, tq=tq, nsa=True),
        grid_spec=grid_spec,
        out_shape=[jax.ShapeDtypeStruct((b, t, npair * SLOT), BF16)] * 2,
        compiler_params=_params(("parallel", "parallel", "arbitrary")),
        name="flash_nsa",
    )(heads, sl_tab, qcx, qcx, selb, selb, slcb, slcb, part, part, g1x, g1x)


DEC_TILES = 4


def _decode_body(pt_ref, q_ref, s_ref, slope_ref, *rest, n_last, nsa):
    n_pg = 2 * DEC_TILES
    k_refs, v_refs = rest[:n_pg], rest[n_pg:2 * n_pg]
    kt_ref, vt_ref = rest[2 * n_pg:2 * n_pg + 2]
    if nsa:
        part_ref, g1_ref, o_ref, m_ref, l_ref, acc_ref = rest[2 * n_pg + 2:]
    else:
        o_ref, m_ref, l_ref, acc_ref = rest[2 * n_pg + 2:]
    step = pl.program_id(1)
    n_steps = n_last // DEC_TILES
    rows = q_ref.shape[1]
    w = q_ref.shape[2]
    selw = s_ref.shape[2]

    @pl.when(step == 0)
    def _():
        m_ref[...] = jnp.full(m_ref.shape, M_INIT, F32)
        l_ref[...] = jnp.zeros(l_ref.shape, F32)
        acc_ref[...] = jnp.zeros(acc_ref.shape, F32)

    def attend(tiles, new_tokens):
        sel_row = lax.broadcasted_iota(jnp.int32, (selw, KV_TILE), 0)
        key = lax.broadcasted_iota(jnp.int32, (selw, KV_TILE), 1)
        qaug = jnp.concatenate([q_ref[0], s_ref[0]], axis=1)
        us, betas = [], []
        for kt, _, n in tiles:
            block = n * (KV_TILE // SLC_BLK) + _div_pow2(key, SLC_BLK) if nsa else n
            ext = jnp.where(sel_row >= selw - N_COEF, key.astype(F32), (sel_row == block).astype(F32)).astype(BF16)
            u = _dot(qaug, jnp.concatenate([kt, ext], axis=0))
            if new_tokens:
                token = lax.broadcasted_iota(jnp.int32, (rows, KV_TILE), 0) & (DEC_ROWS - 1)
                u = jnp.where(lax.broadcasted_iota(jnp.int32, (rows, KV_TILE), 1) <= token, u, NEG_BIG)
            us.append(u)
            betas.append(slope_ref[:, 0:1] * jnp.asarray((n - n_last) * KV_TILE, jnp.int32).astype(F32))
        m = m_ref[...]
        m_new = m
        for u, beta in zip(us, betas):
            m_new = jnp.maximum(m_new, jnp.max(u, axis=1, keepdims=True) + beta)
        alpha = jnp.exp(m - m_new)
        l = alpha * l_ref[...]
        acc = alpha * acc_ref[...]
        for u, beta, (_, vt, _) in zip(us, betas, tiles):
            p = jnp.exp(u - (m_new - beta))
            l = l + jnp.sum(p, axis=1, keepdims=True)
            acc = acc + _dot_nt(p.astype(BF16), vt)
        m_ref[...] = m_new
        l_ref[...] = l
        acc_ref[...] = acc

    def page_tile(refs, j):
        return jnp.concatenate([refs[2 * j][...].reshape(w, PAGE_SIZE), refs[2 * j + 1][...].reshape(w, PAGE_SIZE)],
                               axis=1).astype(BF16)

    @pl.when(step < n_steps)
    def _():
        attend([(page_tile(k_refs, j), page_tile(v_refs, j), step * DEC_TILES + j) for j in range(DEC_TILES)], False)

    @pl.when(step == n_steps)
    def _():
        attend([(kt_ref[0].astype(BF16), vt_ref[0].astype(BF16), n_last)], True)
        o = acc_ref[...] / l_ref[...]
        if nsa:
            o = part_ref[0] + g1_ref[0] * o
        o_ref[0] = o


def _decode(q_rows, s_rows, slope_rows, pages, k_idx, v_idx, tails, page_table, extra, name):
    b, rows, w = q_rows.shape
    selw = s_rows.shape[2]
    n_pages = page_table.shape[1]
    n_last = n_pages * PAGE_SIZE // KV_TILE
    assert n_last % DEC_TILES == 0
    n_pg = 2 * DEC_TILES
    heads = pages.shape[2]
    per_b = lambda width: pl.BlockSpec((1, rows, width), lambda bi, n, pt: (bi, 0, 0))

    def page(comp, k):
        return pl.BlockSpec((1, 1, heads, HEAD_DIM, PAGE_SIZE),
                            lambda bi, n, pt: (pt[bi, jnp.minimum(n_pg * n + k, n_pages - 1)], comp, 0, 0, 0))

    tail = lambda comp: pl.BlockSpec((1, w, KV_TILE), lambda bi, n, pt: (bi, comp, 0))
    grid_spec = pltpu.PrefetchScalarGridSpec(
        num_scalar_prefetch=1,
        grid=(b, n_last // DEC_TILES + 1),
        in_specs=[per_b(w), per_b(selw), pl.BlockSpec(slope_rows.shape, lambda bi, n, pt: (0, 0))]
        + [page(k_idx, k) for k in range(n_pg)] + [page(v_idx, k) for k in range(n_pg)]
        + [tail(k_idx), tail(v_idx)] + [per_b(w) for _ in extra],
        out_specs=per_b(w),
        scratch_shapes=[pltpu.VMEM((rows, 1), F32), pltpu.VMEM((rows, 1), F32), pltpu.VMEM((rows, w), F32)],
    )
    return pl.pallas_call(
        functools.partial(_decode_body, n_last=n_last, nsa=bool(extra)),
        grid_spec=grid_spec,
        out_shape=jax.ShapeDtypeStruct((b, rows, w), F32),
        compiler_params=_params(("parallel", "arbitrary")),
        name=name,
    )(page_table, q_rows, s_rows, slope_rows, *([pages] * (2 * n_pg)), tails, tails, *extra)


KMEAN_BLOCKS = 4


def _paged_kmean_body(pt_ref, *refs):
    *page_refs, km_ref = refs
    step = pl.program_id(1)
    w = km_ref.shape[1]

    @pl.when(step == 0)
    def _():
        km_ref[...] = jnp.zeros(km_ref.shape, F32)

    lane = lax.broadcasted_iota(jnp.int32, (w, SLOT), 1)
    avg = jnp.full((PAGE_SIZE, SLOT), 1.0 / MOBA_BLOCK, BF16)
    km = km_ref[0]
    for j in range(KMEAN_BLOCKS):
        mean = jnp.zeros((w, SLOT), F32)
        for r in page_refs[2 * j:2 * j + 2]:
            hi, lo = _split_bf16(r[...].reshape(w, PAGE_SIZE))
            mean = mean + _dot(hi, avg) + _dot(lo, avg)
        km = jnp.where(lane == step * KMEAN_BLOCKS + j, mean, km)
    km_ref[0] = km


def _paged_kmean(pages, page_table):
    b, n_pages = page_table.shape
    heads = pages.shape[2]
    w = heads * HEAD_DIM
    n_past = n_pages * PAGE_SIZE // MOBA_BLOCK
    assert n_past <= SLOT and n_past % KMEAN_BLOCKS == 0
    n_pg = 2 * KMEAN_BLOCKS

    def page(k):
        return pl.BlockSpec((1, 1, heads, HEAD_DIM, PAGE_SIZE),
                            lambda bi, n, pt: (pt[bi, n_pg * n + k], 0, 0, 0, 0))

    grid_spec = pltpu.PrefetchScalarGridSpec(
        num_scalar_prefetch=1,
        grid=(b, n_past // KMEAN_BLOCKS),
        in_specs=[page(k) for k in range(n_pg)],
        out_specs=pl.BlockSpec((1, w, SLOT), lambda bi, n, pt: (bi, 0, 0)),
    )
    return pl.pallas_call(
        _paged_kmean_body,
        grid_spec=grid_spec,
        out_shape=jax.ShapeDtypeStruct((b, w, SLOT), F32),
        compiler_params=_params(("parallel", "arbitrary")),
        name="paged_kmean",
    )(page_table, *([pages] * n_pg))


def _cmp1_body(rawk_ref, rawv_ref, pos_ref, w_ref, u_ref, *, rows):
    nc = rows // CMP_STRIDE
    acc = [jnp.zeros((nc, SLOT), F32) for _ in range(4)]
    for r in range(CMP_STRIDE):
        for kv, raw_ref in enumerate((rawk_ref, rawv_ref)):
            xs = raw_ref[0, pl.ds(r, nc, stride=CMP_STRIDE), :]
            for half in range(2):
                rr = half * CMP_STRIDE + r
                xp = (xs + pos_ref[rr:rr + 1, kv * SLOT:(kv + 1) * SLOT]).astype(BF16)
                acc[2 * kv + half] = acc[2 * kv + half] + _dot(xp, w_ref[kv, half, r])
    u_ref[0] = jnp.concatenate(acc, axis=1)


def _cmp1(raw, pos_e, w1e, rows):
    b, l_pad, _ = raw.shape
    nc = rows // CMP_STRIDE
    return pl.pallas_call(
        functools.partial(_cmp1_body, rows=rows),
        grid=(b, l_pad // rows),
        in_specs=[
            pl.BlockSpec((1, rows, SLOT), lambda bi, i: (bi, i, 0)),
            pl.BlockSpec((1, rows, SLOT), lambda bi, i: (bi, i, 1)),
            pl.BlockSpec(pos_e.shape, lambda bi, i: (0, 0)),
            pl.BlockSpec(w1e.shape, lambda bi, i: (0, 0, 0, 0, 0)),
        ],
        out_specs=pl.BlockSpec((1, nc, 4 * SLOT), lambda bi, i: (bi, i, 0)),
        out_shape=jax.ShapeDtypeStruct((b, l_pad // CMP_STRIDE, 4 * SLOT), F32),
        compiler_params=_params(("parallel", "parallel")),
        name="cmp1",
    )(raw, raw, pos_e, w1e)


def _cmp2_body(ua_ref, ub_ref, w2_ref, seg_ref, gain_ref, kc_ref, vc_ref):
    ua = ua_ref[0]
    ub = ub_ref[0]
    pre_k = ua[:, 0:SLOT] + ub[:, SLOT:2 * SLOT]
    pre_v = ua[:, 2 * SLOT:3 * SLOT] + ub[:, 3 * SLOT:4 * SLOT]
    ck = _dot(jax.nn.gelu(pre_k).astype(BF16), w2_ref[0])
    cv = _dot(jax.nn.gelu(pre_v).astype(BF16), w2_ref[1])
    kc_ref[0] = _head_rms(ck, seg_ref[...], gain_ref[...]).astype(BF16)
    vc_ref[0] = cv.astype(BF16)


def _cmp2(ua, ub, w2e, seg, gain3):
    b, ncp, _ = ua.shape
    blk = pl.BlockSpec((1, ncp, 4 * SLOT), lambda bi: (bi, 0, 0))
    out = pl.BlockSpec((1, ncp, SLOT), lambda bi: (bi, 0, 0))
    return pl.pallas_call(
        _cmp2_body,
        grid=(b,),
        in_specs=[blk, blk, pl.BlockSpec(w2e.shape, lambda bi: (0, 0, 0)),
                  pl.BlockSpec(seg.shape, lambda bi: (0, 0)), pl.BlockSpec(gain3.shape, lambda bi: (0, 0))],
        out_specs=[out, out],
        out_shape=[jax.ShapeDtypeStruct((b, ncp, SLOT), BF16)] * 2,
        compiler_params=_params(("parallel",)),
        name="cmp2",
    )(ua, ub, w2e, seg, gain3)


def _nsa_sel_body(sl_ref, q_ref, g_ref, kc_ref, vc_ref, ov_ref, wk_ref, wv_ref, part_ref, g1_ref, selb_ref, *,
                  pos0, tq, n_cmp, pw0, lw_valid):
    t0 = pos0 + pl.program_id(1) * tq
    q = q_ref[0]
    gs = g_ref[0]
    ncp = kc_ref.shape[1]
    nsp = ov_ref.shape[1]
    n_wtiles = wk_ref.shape[1] // KV_TILE
    t = t0 + lax.broadcasted_iota(jnp.int32, (tq, 1), 0)
    t3 = jnp.concatenate([t, t, t], axis=0)
    lane = lax.broadcasted_iota(jnp.int32, (tq, SLOT), 1)
    mcol = lax.broadcasted_iota(jnp.int32, (tq, nsp), 1)
    mcol_f = mcol.astype(F32)
    col = lax.broadcasted_iota(jnp.int32, (1, KV_TILE), 1)
    own = _div_pow2(t, SLC_BLK)

    parts, g1s, selbs = [], [], []
    for g in range(2):
        q3 = jnp.concatenate([q[:, (3 * g + r) * SLOT:(3 * g + r + 1) * SLOT] for r in range(3)], axis=0)
        slope3 = jnp.concatenate([jnp.full((tq, 1), sl_ref[3 * g + r], F32) for r in range(3)], axis=0)

        cend = lax.broadcasted_iota(jnp.int32, (1, ncp), 1) * CMP_STRIDE + (CMP_BLK - 1)
        dist = t3 - cend
        s = _dot_nt(q3, kc_ref[0]) - slope3 * dist.astype(F32)
        mask = (dist >= 0) & (cend < n_cmp * CMP_STRIDE + (CMP_BLK - 1))
        p = _masked_softmax_rows(s, mask)
        o_cmp = _dot(p.astype(BF16), vc_ref[0])
        p_hi, p_lo = _split_bf16(p[0:tq] + p[tq:2 * tq] + p[2 * tq:3 * tq])
        imp = _dot(p_hi, ov_ref[...]) + _dot(p_lo, ov_ref[...])

        forced = (mcol == 0) | (mcol == own) | (mcol == own - 1)
        score = jnp.where(forced, FORCE_SCORE, jnp.where(mcol <= own, imp, -jnp.inf))
        sel = _topk_select(score, mcol_f, SLC_TOPK, jnp.zeros((tq, nsp), jnp.bool_))
        selbs.append(jnp.where(sel, 0.0, NEG_BIG).astype(BF16))

        wb = (t0 - pw0) // KV_TILE - 2
        s_parts, v_parts, m_parts = [], [], []
        for j in range(3):
            tile = wb + j
            off = pl.multiple_of(jnp.clip(tile, 0, n_wtiles - 1) * KV_TILE, KV_TILE)
            ridx = tile * KV_TILE + col
            dist = t3 - (pw0 + ridx)
            s_parts.append(_dot_nt(q3, wk_ref[0, pl.ds(off, KV_TILE), :]) - slope3 * dist.astype(F32))
            m_parts.append((ridx >= 0) & (ridx < lw_valid) & (dist >= 0) & (dist <= WINDOW))
            v_parts.append(wv_ref[0, pl.ds(off, KV_TILE), :])
        p = _masked_softmax_rows(jnp.concatenate(s_parts, axis=1), jnp.concatenate(m_parts, axis=1))
        o_win = sum(_dot(p[:, j * KV_TILE:(j + 1) * KV_TILE].astype(BF16), v_parts[j]) for j in range(3))

        in_group = (lane >= HEAD_DIM) if g else (lane < HEAD_DIM)
        for r in range(3):
            hd = 3 * g + r
            rows = slice(r * tq, (r + 1) * tq)
            o = gs[:, 3 * hd:3 * hd + 1] * o_cmp[rows] + gs[:, 3 * hd + 2:3 * hd + 3] * o_win[rows]
            parts.append(jnp.where(in_group, o, 0.0))
            g1s.append(jnp.where(in_group, gs[:, 3 * hd + 1:3 * hd + 2], 0.0))
    part_ref[0] = jnp.concatenate(parts, axis=1)
    g1_ref[0] = jnp.concatenate(g1s, axis=1)
    selb_ref[0] = jnp.concatenate(selbs, axis=1)


def _nsa_sel(qcx, gsig, kc, vc, ov, winb, slopes, pos0, tq, n_cmp, pw0, lw_valid):
    b, nq, wq = qcx.shape
    lw = winb.shape[1]
    ncp = kc.shape[1]
    nsp = ov.shape[1]
    grid_spec = pltpu.PrefetchScalarGridSpec(
        num_scalar_prefetch=1,
        grid=(b, nq // tq),
        in_specs=[
            pl.BlockSpec((1, tq, wq), lambda bi, i, sl: (bi, i, 0)),
            pl.BlockSpec((1, tq, SLOT), lambda bi, i, sl: (bi, i, 0)),
            pl.BlockSpec((1, ncp, SLOT), lambda bi, i, sl: (bi, 0, 0)),
            pl.BlockSpec((1, ncp, SLOT), lambda bi, i, sl: (bi, 0, 0)),
            pl.BlockSpec(ov.shape, lambda bi, i, sl: (0, 0)),
            pl.BlockSpec((1, lw, SLOT), lambda bi, i, sl: (bi, 0, 0)),
            pl.BlockSpec((1, lw, SLOT), lambda bi, i, sl: (bi, 0, 1)),
        ],
        out_specs=[pl.BlockSpec((1, tq, wq), lambda bi, i, sl: (bi, i, 0)),
                   pl.BlockSpec((1, tq, wq), lambda bi, i, sl: (bi, i, 0)),
                   pl.BlockSpec((1, tq, 2 * nsp), lambda bi, i, sl: (bi, i, 0))],
    )
    return pl.pallas_call(
        functools.partial(_nsa_sel_body, pos0=pos0, tq=tq, n_cmp=n_cmp, pw0=pw0, lw_valid=lw_valid),
        grid_spec=grid_spec,
        out_shape=[jax.ShapeDtypeStruct((b, nq, wq), F32), jax.ShapeDtypeStruct((b, nq, wq), F32),
                   jax.ShapeDtypeStruct((b, nq, 2 * nsp), BF16)],
        compiler_params=_params(("parallel", "arbitrary")),
        name="nsa_sel",
    )(slopes, qcx, gsig, kc, vc, ov, winb, winb)


def _conv_body(cin_ref, cw_ref, cb_ref, lg_ref, lb_ref, y_ref, *, tm):
    off = pl.multiple_of(pl.program_id(1) * tm, 8)
    win = cin_ref[0, pl.ds(off, tm + CONV_PAD), :]
    acc = jnp.zeros((tm, win.shape[1]), F32)
    for w in range(CONV_W):
        lo = w + CONV_PAD - (CONV_W - 1)
        acc = acc + cw_ref[w:w + 1, :] * win[lo:lo + tm, :]
    acc = acc + cb_ref[...]
    mu = jnp.mean(acc, axis=-1, keepdims=True)
    cen = acc - mu
    var = jnp.mean(cen * cen, axis=-1, keepdims=True)
    yn = cen * lax.rsqrt(var + EPS) * lg_ref[...] + lb_ref[...]
    y_ref[0] = (yn * _sigmoid(yn)).astype(BF16)


def _conv(cin, cw, cb, lg, lb, tm):
    b, rows, ch = cin.shape
    t_pad = rows - CONV_PAD
    full2 = lambda a: pl.BlockSpec(a.shape, lambda bi, i: (0, 0))
    return pl.pallas_call(
        functools.partial(_conv_body, tm=tm),
        grid=(b, t_pad // tm),
        in_specs=[pl.BlockSpec((1, rows, ch), lambda bi, i: (bi, 0, 0)), full2(cw), full2(cb), full2(lg), full2(lb)],
        out_specs=pl.BlockSpec((1, tm, ch), lambda bi, i: (bi, i, 0)),
        out_shape=jax.ShapeDtypeStruct((b, t_pad, ch), BF16),
        compiler_params=_params(("parallel", "arbitrary")),
        name="conv",
    )(cin, cw, cb, lg, lb)


def _ffn_body(x_ref, oa_ref, y_ref, oc0_ref, oc1_ref, wa_ref, wy_ref, wc0_ref, wc1_ref, g2_ref, wu_ref, wd_ref,
              out_ref):
    x1 = (x_ref[...] + _dot(oa_ref[...], wa_ref[...]) + _dot(y_ref[...], wy_ref[...])
          + _dot(oc0_ref[...], wc0_ref[...]) + _dot(oc1_ref[...], wc1_ref[...]))
    ms = jnp.mean(x1 * x1, axis=-1, keepdims=True)
    h2 = (x1 * lax.rsqrt(ms + EPS) * g2_ref[...]).astype(BF16)
    u = jnp.maximum(_dot(h2, wu_ref[...]), 0.0)
    out_ref[...] = x1 + _dot((u * u).astype(BF16), wd_ref[...])


def _ffn(x2d, acts, weights, tm):
    m, d = x2d.shape
    row = lambda a: pl.BlockSpec((tm, a.shape[1]), lambda i: (i, 0))
    const = lambda a: pl.BlockSpec(a.shape, lambda i: (0, 0), pipeline_mode=pl.Buffered(1))
    return pl.pallas_call(
        _ffn_body,
        grid=(m // tm,),
        in_specs=[row(x2d)] + [row(a) for a in acts] + [const(w) for w in weights],
        out_specs=pl.BlockSpec((tm, d), lambda i: (i, 0)),
        out_shape=jax.ShapeDtypeStruct((m, d), F32),
        compiler_params=_params(("parallel",)),
        name="out_ffn",
    )(x2d, *acts, *weights)


GATHER_PAGES = 4


def _gather_rows_body(pt_ref, *refs, n_past):
    *page_refs, tail_ref, out_ref = refs
    w = out_ref.shape[2]
    blk = jnp.concatenate([r[...].reshape(w, PAGE_SIZE).T for r in page_refs], axis=0)
    out_ref[0] = jnp.where(pl.program_id(1) < n_past, blk, tail_ref[0])


def _gather_rows(pages, n_comp, tail, page_table):
    bsz, n_pages = page_table.shape
    heads = pages.shape[2]
    w = n_comp * heads * HEAD_DIM
    assert n_pages % GATHER_PAGES == 0 and w == tail.shape[2]
    n_past = n_pages // GATHER_PAGES
    rows = GATHER_PAGES * PAGE_SIZE

    def page(k):
        return pl.BlockSpec((1, n_comp, heads, HEAD_DIM, PAGE_SIZE),
                            lambda bi, n, pt: (pt[bi, jnp.minimum(GATHER_PAGES * n + k, n_pages - 1)], 0, 0, 0, 0))

    grid_spec = pltpu.PrefetchScalarGridSpec(
        num_scalar_prefetch=1,
        grid=(bsz, n_past + 1),
        in_specs=[page(k) for k in range(GATHER_PAGES)] + [pl.BlockSpec((1, rows, w), lambda bi, n, pt: (bi, 0, 0))],
        out_specs=pl.BlockSpec((1, rows, w), lambda bi, n, pt: (bi, n, 0)),
    )
    return pl.pallas_call(
        functools.partial(_gather_rows_body, n_past=n_past),
        grid_spec=grid_spec,
        out_shape=jax.ShapeDtypeStruct((bsz, (n_past + 1) * rows, w), F32),
        compiler_params=_params(("parallel", "arbitrary")),
        name="gather_rows",
    )(page_table, *([pages] * GATHER_PAGES), tail)


def _layer_consts(lp):
    (ln1_g, w_in, qk_gain, cmp_pos, cmp_w1, cmp_w2, conv_w, conv_b, conv_ln_g, conv_ln_b, w_out, ln2_g, w_up,
     w_down) = lp
    d = w_in.shape[0]
    z64 = jnp.zeros((d, HEAD_DIM), F32)
    qc = w_in[:, 1664:2048].reshape(d, 6, HEAD_DIM)
    qc_slots = [jnp.concatenate([qc[:, h], z64] if h < 3 else [z64, qc[:, h]], axis=1) for h in range(6)]
    gate = jnp.pad(w_in[:, 2816:2834], ((0, 0), (0, SLOT - 18)))
    w_re = jnp.concatenate([w_in[:, 0:1664]] + qc_slots + [w_in[:, 2048:2816], gate], axis=1).astype(BF16)

    ones = lambda n: jnp.ones((n,), F32)
    g = qk_gain
    gain_row = jnp.concatenate([
        jnp.tile(g[0], 6) * SCALE, jnp.tile(g[1], 6), ones(C_QC - C_VA),
        jnp.tile(g[2], 12) * SCALE, ones(2 * SLOT), jnp.tile(g[4], 2), ones(SLOT),
        jnp.tile(g[5], 2), ones(SLOT), ones(SLOT)])[None, :]
    seg = jnp.asarray(np.kron(np.eye(2, dtype=np.float32), np.full((HEAD_DIM, HEAD_DIM), 1.0 / HEAD_DIM, np.float32)),
                      BF16)

    eye2 = jnp.eye(2, dtype=F32)
    w1 = cmp_w1.reshape(2, 2, CMP_STRIDE, HEAD_DIM, HEAD_DIM)
    w1e = jnp.einsum("ab,khrdj->khradbj", eye2, w1).reshape(2, 2, CMP_STRIDE, SLOT, SLOT).astype(BF16)
    w2e = jnp.einsum("ab,kdj->kadbj", eye2, cmp_w2).reshape(2, SLOT, SLOT).astype(BF16)
    pos_e = jnp.concatenate([jnp.tile(cmp_pos[0], (1, 2)), jnp.tile(cmp_pos[1], (1, 2))], axis=1)
    gain3 = jnp.tile(g[3], 2)[None, :]

    wo = w_out
    wa = wo[0:384].astype(BF16)
    wy = wo[384:640].astype(BF16)
    wc_rows = wo[640:1024].reshape(6, HEAD_DIM, d)
    zr = jnp.zeros((HEAD_DIM, d), F32)
    wc = jnp.concatenate([jnp.concatenate([wc_rows[h], zr] if h < 3 else [zr, wc_rows[h]], axis=0)
                          for h in range(6)], axis=0).astype(BF16)
    ffn_w = (wa, wy, wc[:3 * SLOT], wc[3 * SLOT:], ln2_g[None, :], w_up.astype(BF16), w_down.astype(BF16))
    return dict(ln1=ln1_g[None, :], w_re=w_re, gain_row=gain_row, seg=seg, w1e=w1e, w2e=w2e, pos_e=pos_e,
                gain3=gain3, conv_w=conv_w, conv_b=conv_b[None, :], conv_lg=conv_ln_g[None, :],
                conv_lb=conv_ln_b[None, :], ffn_w=ffn_w)


def _overlap_matrix(ncp, nsp, n_cmp, n_slc):
    n = np.arange(ncp)[:, None]
    m = np.arange(nsp)[None, :]
    ov = ((m == (n * CMP_STRIDE) // SLC_BLK) | (m == (n * CMP_STRIDE + CMP_BLK - 1) // SLC_BLK))
    ov = ov & (n < n_cmp) & (m < n_slc)
    return jnp.asarray(ov.astype(np.float32), BF16)


def _round_up(x, m):
    return -(-x // m) * m


def _pick_rows(l_pad):
    for rows in (8448, 8192, 4096, 2048, 1280, 1024, 768, 512, 256):
        if l_pad % rows == 0:
            return rows
    raise ValueError(l_pad)


def _head_rows(a, t_rows):
    b = a.shape[0]
    w = a.shape[2] // N_HEADS
    return a.reshape(b, t_rows, N_HEADS, w).transpose(0, 2, 1, 3).reshape(b, N_HEADS * t_rows, w)


def _with_coefs(s_rows, sl_tab, n_blocks):
    selw = s_rows.shape[2]
    assert n_blocks <= selw - N_COEF
    coef = np.zeros((N_HEADS * DEC_ROWS, selw), np.float32)
    coef[:, selw - N_COEF:] = np.repeat(sl_tab[:, :N_COEF], DEC_ROWS, axis=0)
    lane = np.arange(selw)[None, :] >= selw - N_COEF
    return jnp.where(jnp.asarray(lane)[None], jnp.asarray(coef, BF16)[None], s_rows)


def _layer(x, c, past, tabs):
    b, t, d = x.shape
    m = b * t
    tm = 256 if m % 256 == 0 else m
    sl_np = tabs["sl_np"]
    qa, kva, kvab, glu, qcx, kvc, slcb, win, winb, gsig, kmean_t = _proj(
        x.reshape(m, d), c["ln1"], c["w_re"], c["seg"], c["gain_row"], tm)
    r3 = lambda a: a.reshape(b, t, a.shape[-1])
    qa, kva, kvab, glu, qcx, kvc, slcb, win, winb, gsig = map(r3, (qa, kva, kvab, glu, qcx, kvc, slcb, win, winb, gsig))
    decode = past is not None

    if not decode:
        past_len = 0
        tq = KV_TILE
        assert t % KV_TILE == 0
        l_pad = t
        nb = t // MOBA_BLOCK
        kmean = kmean_t[:, 0, :].reshape(b, nb, 384)
        raw = kvc
        win_all = winb
        lw_valid = t
        pw0 = 0
        q_a, q_c, g_c = qa, qcx, gsig
        new_win_state = win[:, -min(WINDOW, t):]
        conv_in = jnp.pad(glu, ((0, 0), (CONV_PAD, 0), (0, 0)))
        new_conv = glu[:, -(CONV_W - 1):]
        conv_tm = 256
    else:
        page_table = past["page_table"]
        past_len = page_table.shape[1] * PAGE_SIZE
        assert past_len % KV_TILE == 0 and t <= DEC_ROWS
        tq = DEC_ROWS
        l_pad = past_len + KV_TILE
        nb = l_pad // MOBA_BLOCK
        to_tail = lambda a: jnp.pad(a, ((0, 0), (0, KV_TILE - t), (0, 0))).transpose(0, 2, 1)
        tail_a, tail_c = to_tail(kva), to_tail(kvc)
        kmean = _paged_kmean(past["moba_t"], page_table).transpose(0, 2, 1)[:, :nb]
        raw_tail = jnp.pad(kvc[:, :, :2 * SLOT], ((0, 0), (0, GATHER_PAGES * PAGE_SIZE - t), (0, 0)))
        raw = _gather_rows(past["nsa_t"], 2, raw_tail, page_table)
        win_buf = past["win"]
        nbuf = win_buf.shape[1]
        lw_valid = nbuf + t
        win_cat = jnp.concatenate([win_buf, win], axis=1)
        win_all = jnp.pad(win_cat, ((0, 0), (0, _round_up(lw_valid, KV_TILE) - lw_valid), (0, 0))).astype(BF16)
        pw0 = past_len - nbuf
        padq = lambda a: jnp.pad(a, ((0, 0), (0, tq - t), (0, 0)))
        q_a, q_c, g_c = padq(qa), padq(qcx), padq(gsig)
        new_win_state = win_cat[:, -nbuf:]
        conv_cat = jnp.concatenate([past["conv"], glu], axis=1)
        conv_in = jnp.pad(conv_cat, ((0, 0), (CONV_PAD - (CONV_W - 1), 8 - t), (0, 0)))
        new_conv = conv_cat[:, -(CONV_W - 1):]
        conv_tm = 8

    n_cmp = l_pad // CMP_STRIDE - 1
    n_slc = l_pad // SLC_BLK
    ncp = _round_up(n_cmp, SLOT)
    nsp = _round_up(n_slc + (N_COEF if decode else 0), SLOT)
    u = _cmp1(raw, c["pos_e"], c["w1e"], _pick_rows(raw.shape[1]))
    ua = jnp.pad(u[:, :n_cmp], ((0, 0), (0, ncp - n_cmp), (0, 0)))
    ub = jnp.pad(u[:, 1:n_cmp + 1], ((0, 0), (0, ncp - n_cmp), (0, 0)))
    kc, vc = _cmp2(ua, ub, c["w2e"], c["seg"], c["gain3"])
    ov = _overlap_matrix(ncp, nsp, n_cmp, n_slc)
    part, g1x, selb_c = _nsa_sel(q_c, g_c, kc, vc, ov, win_all, tabs["slopes"], past_len, tq, n_cmp, pw0, lw_valid)

    kmean = jnp.pad(kmean, ((0, 0), (0, SLOT - nb), (0, 0)))
    selb_a = _moba_gate(q_a, kmean, past_len, tq)

    if not decode:
        o_a = _flash_moba(q_a, selb_a, kvab, tabs["heads_a"], tabs["sl_tab"])
        oc0, oc1 = _flash_nsa(q_c, selb_c, slcb, part, g1x, tabs["heads_c"], tabs["sl_tab"])
    else:
        slope_rows =jnp.asarray(np.repeat(sl_np[:, N_COEF:], DEC_ROWS, axis=0) * np.ones((1, SLOT), np.float32))
        head_of_lane = np.arange(N_HEADS * HEAD_DIM)[None, :] // HEAD_DIM == np.arange(N_HEADS)[:, None]
        q_rows = jnp.where(jnp.asarray(head_of_lane)[None, :, None, :], q_a[:, None], jnp.zeros((), BF16))
        q_rows = q_rows.reshape(b, N_HEADS * DEC_ROWS, q_a.shape[2])
        s_rows = _with_coefs(_head_rows(selb_a, DEC_ROWS), sl_np, nb)
        o_rows = _decode(q_rows, s_rows, slope_rows, past["moba_t"], 0, 1, tail_a, page_table, (), "decode_moba")
        o_a = jnp.concatenate([o_rows[:, h * DEC_ROWS:h * DEC_ROWS + t, h * HEAD_DIM:(h + 1) * HEAD_DIM]
                               for h in range(N_HEADS)], axis=2).astype(BF16)
        sel_g = selb_c.reshape(b, DEC_ROWS, 2, nsp)
        s_rows = jnp.concatenate([sel_g[:, :, h // 3] for h in range(N_HEADS)], axis=1)
        s_rows = _with_coefs(s_rows, sl_np, n_slc)
        o_rows = _decode(_head_rows(q_c, DEC_ROWS), s_rows, slope_rows, past["nsa_t"], 2, 3, tail_c, page_table,
                         (_head_rows(part, DEC_ROWS), _head_rows(g1x, DEC_ROWS)), "decode_nsa")
        o_c = o_rows.reshape(b, N_HEADS, DEC_ROWS, SLOT).transpose(0, 2, 1, 3).reshape(b, DEC_ROWS, N_HEADS * SLOT)
        o_c = o_c[:, :t].astype(BF16)
        oc0, oc1 = o_c[:, :, :3 * SLOT], o_c[:, :, 3 * SLOT:]

    y = _conv(conv_in, c["conv_w"], c["conv_b"], c["conv_lg"], c["conv_lb"], conv_tm)[:, :t]

    f2 = lambda a: a.reshape(m, a.shape[-1])
    x_new = _ffn(x.reshape(m, d), (f2(o_a), f2(y), f2(oc0), f2(oc1)), c["ffn_w"], tm)
    return x_new.reshape(b, t, d), kva, kvc, new_win_state, new_conv


def _tables():
    sl_np = _slope_table(N_HEADS)
    return dict(sl_np=sl_np, sl_tab=jnp.asarray(sl_np), slopes=jnp.asarray(sl_np[:, N_COEF]),
                heads_a=jnp.asarray(np.array([[0, 1], [2, 3], [4, 5]], np.int32)),
                heads_c=jnp.asarray(np.array([[0, 3], [1, 4], [2, 5]], np.int32)))


def kernel(x_prompt, x_sample, cache_moba_kv, cache_nsa_kv, cache_win_kv, cache_conv, page_table, ln1_g, w_in,
           qk_gain, cmp_pos, cmp_w1, cmp_w2, conv_w, conv_b, conv_ln_g, conv_ln_b, w_out, ln2_g, w_up, w_down):
    depth = w_in.shape[0]
    bp, tp, _ = x_prompt.shape
    bs, ts, _ = x_sample.shape
    n_pool = cache_moba_kv.shape[1]
    tabs = _tables()
    yp, ys = x_prompt, x_sample
    outs = [[] for _ in range(8)]
    for l in range(depth):
        c = _layer_consts((ln1_g[l], w_in[l], qk_gain[l], cmp_pos[l], cmp_w1[l], cmp_w2[l], conv_w[l], conv_b[l],
                           conv_ln_g[l], conv_ln_b[l], w_out[l], ln2_g[l], w_up[l], w_down[l]))
        yp, a, cc, w, cv = _layer(yp, c, None, tabs)
        outs[0].append(a.reshape(bp, tp, 2, 6, HEAD_DIM))
        outs[2].append(cc.reshape(bp, tp, 4, 2, HEAD_DIM))
        outs[4].append(w.reshape(bp, w.shape[1], 2, 2, HEAD_DIM))
        outs[6].append(cv)
        past = dict(page_table=page_table + l * n_pool,
                    moba_t=cache_moba_kv.transpose(0, 1, 3, 4, 5, 2).reshape(depth * n_pool, 2, 6, HEAD_DIM, PAGE_SIZE),
                    nsa_t=cache_nsa_kv.transpose(0, 1, 3, 4, 5, 2).reshape(depth * n_pool, 4, 2, HEAD_DIM, PAGE_SIZE),
                    win=cache_win_kv[l].reshape(bs, cache_win_kv.shape[2], 256),
                    conv=cache_conv[l])
        ys, a, cc, w, cv = _layer(ys, c, past, tabs)
        outs[1].append(a.reshape(bs, ts, 2, 6, HEAD_DIM))
        outs[3].append(cc.reshape(bs, ts, 4, 2, HEAD_DIM))
        outs[5].append(w.reshape(bs, w.shape[1], 2, 2, HEAD_DIM))
        outs[7].append(cv)
    return (yp, ys) + tuple(jnp.stack(o) for o in outs)
```

```python
import functools

import numpy as np
import jax
import jax.numpy as jnp
from jax import lax
from jax.experimental import pallas as pl
from jax.experimental.pallas import tpu as pltpu

F32 = jnp.float32
BF16 = jnp.bfloat16

HEAD_DIM = 64
SLOT = 2 * HEAD_DIM
N_HEADS = 6
PAGE_SIZE = 128
MOBA_BLOCK = 256
MOBA_TOPK = 3
CMP_STRIDE = 16
CMP_BLK = 32
SLC_BLK = 64
SLC_TOPK = 16
WINDOW = 512
CONV_W = 31
CONV_PAD = 32
EPS = 1e-6
FORCE_SCORE = 1e4
SCALE = HEAD_DIM ** -0.5
NEG_BIG = -(2.0 ** 100)
M_INIT = -(2.0 ** 99)
KV_TILE = 256
DEC_ROWS = 16
N_COEF = 3
VMEM_LIMIT = 56 * 1024 * 1024

C_QA, C_KA, C_VA, C_GLU, C_QC, C_KVC, C_WIN, C_GATE, C_END = 0, 384, 768, 1152, 1664, 2432, 2944, 3200, 3328


def _alibi_slopes(n):
    return np.array([2.0 ** (-8.0 * (i + 1) / n) for i in range(n)], dtype=np.float32)


def _slope_table(n):
    sl = _alibi_slopes(n)
    terms, rest = [], sl.copy()
    for _ in range(N_COEF):
        term = rest.astype(BF16).astype(np.float32)
        terms.append(term)
        rest = rest - term
    assert not rest.any()
    return np.stack(terms + [sl], axis=1)


def _dot(a, b):
    return jnp.dot(a, b, preferred_element_type=F32)


def _dot_nt(a, b):
    return lax.dot_general(a, b, (((1,), (1,)), ((), ())), preferred_element_type=F32)


def _split_bf16(x):
    hi = x.astype(BF16)
    lo = (x - hi.astype(F32)).astype(BF16)
    return hi, lo


def _div_pow2(x, d):
    return lax.shift_right_logical(x, jnp.int32(int(d).bit_length() - 1))


def _sigmoid(x):
    return 1.0 / (1.0 + jnp.exp(-x))


def _params(sem, vmem=VMEM_LIMIT):
    return pltpu.CompilerParams(dimension_semantics=sem, vmem_limit_bytes=vmem)


def _head_rms(zs, seg, gain):
    cols = []
    for c in range(zs.shape[1] // SLOT):
        zz = zs[:, c * SLOT:(c + 1) * SLOT]
        hi, lo = _split_bf16(zz * zz)
        cols.append(_dot(hi, seg) + _dot(lo, seg))
    ms = cols[0] if len(cols) == 1 else jnp.concatenate(cols, axis=1)
    return zs * lax.rsqrt(ms + EPS) * gain


def _proj_body(x_ref, ln_ref, w_ref, seg_ref, gain_ref,
               qa_ref, kva_ref, kvab_ref, glu_ref, qcx_ref, kvc_ref, slcb_ref, win_ref, winb_ref,
               gsig_ref, kmean_ref):
    x = x_ref[...]
    ms = jnp.mean(x * x, axis=-1, keepdims=True)
    h = (x * lax.rsqrt(ms + EPS) * ln_ref[...]).astype(BF16)
    z = _dot(h, w_ref[...])
    seg = seg_ref[...]

    def normed(lo, hi):
        return _head_rms(z[:, lo:hi], seg, gain_ref[:, lo:hi])

    qa_ref[...] = normed(C_QA, C_KA).astype(BF16)
    ka = normed(C_KA, C_VA)
    kva = jnp.concatenate([ka, z[:, C_VA:C_GLU]], axis=1)
    kva_ref[...] = kva
    kvab_ref[...] = kva.astype(BF16)
    kmean_ref[0] = jnp.broadcast_to(jnp.sum(ka, axis=0, keepdims=True) * (1.0 / ka.shape[0]), (8, ka.shape[1]))
    half = (C_QC - C_GLU) // 2
    glu_ref[...] = z[:, C_GLU:C_GLU + half] * _sigmoid(z[:, C_GLU + half:C_QC])
    qcx_ref[...] = normed(C_QC, C_KVC).astype(BF16)
    slck = normed(C_KVC + 2 * SLOT, C_KVC + 3 * SLOT)
    kvc = jnp.concatenate([z[:, C_KVC:C_KVC + 2 * SLOT], slck, z[:, C_KVC + 3 * SLOT:C_WIN]], axis=1)
    kvc_ref[...] = kvc
    slcb_ref[...] = kvc[:, 2 * SLOT:].astype(BF16)
    wink = normed(C_WIN, C_WIN + SLOT)
    win = jnp.concatenate([wink, z[:, C_WIN + SLOT:C_GATE]], axis=1)
    win_ref[...] = win
    winb_ref[...] = win.astype(BF16)
    gsig_ref[...] = _sigmoid(z[:, C_GATE:C_END])


def _proj(x2d, ln_g, w_re, seg, gain_row, tm):
    m, d = x2d.shape
    nt = m // tm
    row = lambda w: pl.BlockSpec((tm, w), lambda i: (i, 0))
    full = lambda a: pl.BlockSpec(a.shape, lambda i: (0,) * a.ndim)
    outs = [(384, BF16), (768, F32), (768, BF16), (256, F32), (768, BF16), (512, F32), (256, BF16),
            (256, F32), (256, BF16), (128, F32)]
    return pl.pallas_call(
        _proj_body,
        grid=(nt,),
        in_specs=[row(d), full(ln_g), full(w_re), full(seg), full(gain_row)],
        out_specs=[row(w) for w, _ in outs] + [pl.BlockSpec((1, 8, 384), lambda i: (i, 0, 0))],
        out_shape=[jax.ShapeDtypeStruct((m, w), dt) for w, dt in outs]
        + [jax.ShapeDtypeStruct((nt, 8, 384), F32)],
        compiler_params=_params(("parallel",)),
        name="proj",
    )(x2d, ln_g, w_re, seg, gain_row)


def _topk_select(score, lane_f, k, sel):
    for _ in range(k):
        mx = jnp.max(score, axis=1, keepdims=True)
        idx = jnp.min(jnp.where(score == mx, lane_f, 1e9), axis=1, keepdims=True)
        hit = lane_f == idx
        sel = sel | (hit & (mx > -jnp.inf))
        score = jnp.where(hit, -jnp.inf, score)
    return sel


def _masked_softmax_rows(s, mask):
    sm = jnp.where(mask, s, NEG_BIG)
    m = jnp.maximum(jnp.max(sm, axis=1, keepdims=True), M_INIT)
    e = jnp.where(mask, jnp.exp(sm - m), 0.0)
    l = jnp.sum(e, axis=1, keepdims=True)
    return e / jnp.where(l > 0.0, l, 1.0)


def _moba_gate_body(q_ref, km_ref, s_ref, *, pos0, tq):
    t0 = pos0 + pl.program_id(1) * tq
    lane = lax.broadcasted_iota(jnp.int32, (tq, SLOT), 1)
    lane_f = lane.astype(F32)
    own = _div_pow2(t0 + lax.broadcasted_iota(jnp.int32, (tq, 1), 0), MOBA_BLOCK)
    for h in range(N_HEADS):
        cols = slice((h // 2) * SLOT, (h // 2 + 1) * SLOT)
        q = q_ref[0, :, cols]
        mine = (lane >= HEAD_DIM) if h % 2 else (lane < HEAD_DIM)
        qh = jnp.where(mine, q, jnp.zeros_like(q))
        km_hi, km_lo = _split_bf16(km_ref[0, :, cols])
        gate = _dot_nt(qh, km_hi) + _dot_nt(qh, km_lo)
        gate = jnp.where(lane < own, gate, -jnp.inf)
        sel = _topk_select(gate, lane_f, MOBA_TOPK, lane == own)
        s_ref[0, :, h * SLOT:(h + 1) * SLOT] = jnp.where(sel, 0.0, NEG_BIG).astype(BF16)


def _moba_gate(q, kmean, pos0, tq):
    b, nq, wa = q.shape
    return pl.pallas_call(
        functools.partial(_moba_gate_body, pos0=pos0, tq=tq),
        grid=(b, nq // tq),
        in_specs=[pl.BlockSpec((1, tq, wa), lambda bi, i: (bi, i, 0)),
                  pl.BlockSpec((1, SLOT, wa), lambda bi, i: (bi, 0, 0))],
        out_specs=pl.BlockSpec((1, tq, N_HEADS * SLOT), lambda bi, i: (bi, i, 0)),
        out_shape=jax.ShapeDtypeStruct((b, nq, N_HEADS * SLOT), BF16),
        compiler_params=_params(("parallel", "arbitrary")),
        name="moba_gate",
    )(q, kmean)


def _flash_body(hd_ref, sl_ref, qa_ref, qb_ref, sa_ref, sb_ref, k_ref, v_ref, *rest, tq, nsa):
    if nsa:
        pa_ref, pb_ref, ga_ref, gb_ref, oa_ref, ob_ref, acc_ref = rest
    else:
        o_ref, acc_ref = rest
    pair = pl.program_id(1)
    i = pl.program_id(2)
    selw = sa_ref.shape[2]
    lane = lax.broadcasted_iota(jnp.int32, (tq, SLOT), 1)
    lower = lane < HEAD_DIM
    lane_k = lax.broadcasted_iota(jnp.int32, (KV_TILE, SLOT), 1)
    klower = lane_k < HEAD_DIM
    row_kf = lax.broadcasted_iota(jnp.int32, (KV_TILE, SLOT), 0).astype(F32)
    lane_s = lax.broadcasted_iota(jnp.int32, (KV_TILE, selw), 1)
    blk_of_row = _div_pow2(lax.broadcasted_iota(jnp.int32, (KV_TILE, selw), 0), SLC_BLK)
    causal = lax.broadcasted_iota(jnp.int32, (tq, KV_TILE), 1) <= lax.broadcasted_iota(jnp.int32, (tq, KV_TILE), 0)
    ones = jnp.ones((KV_TILE, SLOT), BF16)

    qaug, slope, kpos, mine_k = [], [], [], []
    for x, (q_ref, s_ref) in enumerate(((qa_ref, sa_ref), (qb_ref, sb_ref))):
        h = hd_ref[pair, x]
        base = 0 if x else HEAD_DIM
        coef = jnp.zeros((tq, SLOT), F32)
        for j in range(N_COEF):
            coef = jnp.where(lane == base + j, sl_ref[h, j], coef)
        mine = (lane >= HEAD_DIM) if x else lower
        qx = jnp.where(mine, q_ref[0].astype(F32), coef).astype(BF16)
        qaug.append(jnp.concatenate([qx, s_ref[0]], axis=1))
        slope.append(sl_ref[h, N_COEF])
        in_coef = (lane_k >= base) & (lane_k < base + N_COEF)
        kpos.append(jnp.where(in_coef, row_kf, 0.0).astype(BF16))
        mine_k.append((lane_k >= HEAD_DIM) if x else klower)
        acc_ref[x] = jnp.zeros((tq, SLOT), F32)

    def scores(n):
        off = pl.multiple_of(n * KV_TILE, KV_TILE)
        k = k_ref[0, pl.ds(off, KV_TILE), :]
        block = n * (KV_TILE // SLC_BLK) + blk_of_row if nsa else n
        onehot = (lane_s == block).astype(BF16)
        return tuple(_dot_nt(qaug[x], jnp.concatenate([jnp.where(mine_k[x], k, kpos[x]), onehot], axis=1))
                     for x in range(2))

    def consume(n, us, ms, masked):
        v = v_ref[0, pl.ds(pl.multiple_of(n * KV_TILE, KV_TILE), KV_TILE), :]
        tile_off = ((n - i) * KV_TILE).astype(F32)
        out = []
        for x in range(2):
            u = jnp.where(causal, us[x], NEG_BIG) if masked else us[x]
            beta = slope[x] * tile_off
            m_new = jnp.maximum(ms[x], jnp.max(u, axis=1, keepdims=True) + beta)
            alpha = jnp.exp(ms[x] - m_new)
            p = jnp.exp(u - (m_new - beta))
            acc_ref[x] = alpha * acc_ref[x] + _dot(p.astype(BF16), jnp.where(mine_k[x], v, ones))
            out.append(m_new)
        return tuple(out)

    def body(n, carry):
        ms, us = carry
        us_next = scores(n + 1)
        return consume(n, us, ms, False), us_next

    m0 = jnp.full((tq, 1), M_INIT, F32)
    ms, us = lax.fori_loop(0, i, body, ((m0, m0), scores(jnp.int32(0))))
    consume(i, us, ms, True)

    acc_a = acc_ref[0]
    acc_b = acc_ref[1]
    o_a = acc_a / acc_a[:, HEAD_DIM:HEAD_DIM + 1]
    o_b = acc_b / acc_b[:, 0:1]
    if nsa:
        oa_ref[0] = (pa_ref[0] + ga_ref[0] * o_a).astype(BF16)
        ob_ref[0] = (pb_ref[0] + gb_ref[0] * o_b).astype(BF16)
    else:
        o_ref[0] = jnp.where(lower, o_a, o_b).astype(BF16)


def _flash_moba(q, selb, kvb, heads, sl_tab):
    b, t, wa = q.shape
    npair = wa // SLOT
    tq = KV_TILE
    slot = lambda f: pl.BlockSpec((1, tq, SLOT), lambda bi, p, i, hd, sl: (bi, i, f(p)))
    kv = lambda f: pl.BlockSpec((1, t, SLOT), lambda bi, p, i, hd, sl: (bi, 0, f(p)))
    grid_spec = pltpu.PrefetchScalarGridSpec(
        num_scalar_prefetch=2,
        grid=(b, npair, t // tq),
        in_specs=[slot(lambda p: p), slot(lambda p: p), slot(lambda p: 2 * p), slot(lambda p: 2 * p + 1),
                  kv(lambda p: p), kv(lambda p: npair + p)],
        out_specs=slot(lambda p: p),
        scratch_shapes=[pltpu.VMEM((2, tq, SLOT), F32)],
    )
    return pl.pallas_call(
        functools.partial(_flash_body, tq=tq, nsa=False),
        grid_spec=grid_spec,
        out_shape=jax.ShapeDtypeStruct((b, t, wa), BF16),
        compiler_params=_params(("parallel", "parallel", "arbitrary")),
        name="flash_moba",
    )(heads, sl_tab, q, q, selb, selb, kvb, kvb)


def _flash_nsa(qcx, selb, slcb, part, g1x, heads, sl_tab):
    b, t, wq = qcx.shape
    npair = wq // SLOT // 2
    nsp = selb.shape[2] // 2
    tq = KV_TILE
    slot = lambda f: pl.BlockSpec((1, tq, SLOT), lambda bi, p, i, hd, sl: (bi, i, f(p)))
    sel = lambda g: pl.BlockSpec((1, tq, nsp), lambda bi, p, i, hd, sl: (bi, i, g))
    kv = lambda c: pl.BlockSpec((1, t, SLOT), lambda bi, p, i, hd, sl: (bi, 0, c))
    lo, hi = (lambda p: p), (lambda p: npair + p)
    grid_spec = pltpu.PrefetchScalarGridSpec(
        num_scalar_prefetch=2,
        grid=(b, npair, t // tq),
        in_specs=[slot(lo), slot(hi), sel(0), sel(1), kv(0), kv(1), slot(lo), slot(hi), slot(lo), slot(hi)],
        out_specs=[slot(lo), slot(lo)],
        scratch_shapes=[pltpu.VMEM((2, tq, SLOT), F32)],
    )
    return pl.pallas_call(
        functools.partial(_flash_body, tq=tq, nsa=True),
        grid_spec=grid_spec,
        out_shape=[jax.ShapeDtypeStruct((b, t, npair * SLOT), BF16)] * 2,
        compiler_params=_params(("parallel", "parallel", "arbitrary")),
        name="flash_nsa",
    )(heads, sl_tab, qcx, qcx, selb, selb, slcb, slcb, part, part, g1x, g1x)


DEC_TILES = 4


def _decode_body(pt_ref, q_ref, s_ref, slope_ref, *rest, n_last, nsa):
    n_pg = 2 * DEC_TILES
    k_refs, v_refs = rest[:n_pg], rest[n_pg:2 * n_pg]
    kt_ref, vt_ref = rest[2 * n_pg:2 * n_pg + 2]
    if nsa:
        part_ref, g1_ref, o_ref, m_ref, l_ref, acc_ref = rest[2 * n_pg + 2:]
    else:
        o_ref, m_ref, l_ref, acc_ref = rest[2 * n_pg + 2:]
    step = pl.program_id(1)
    n_steps = n_last // DEC_TILES
    rows = q_ref.shape[1]
    w = q_ref.shape[2]
    selw = s_ref.shape[2]

    @pl.when(step == 0)
    def _():
        m_ref[...] = jnp.full(m_ref.shape, M_INIT, F32)
        l_ref[...] = jnp.zeros(l_ref.shape, F32)
        acc_ref[...] = jnp.zeros(acc_ref.shape, F32)

    def attend(tiles, new_tokens):
        sel_row = lax.broadcasted_iota(jnp.int32, (selw, KV_TILE), 0)
        key = lax.broadcasted_iota(jnp.int32, (selw, KV_TILE), 1)
        qaug = jnp.concatenate([q_ref[0], s_ref[0]], axis=1)
        us, betas = [], []
        for kt, _, n in tiles:
            block = n * (KV_TILE // SLC_BLK) + _div_pow2(key, SLC_BLK) if nsa else n
            ext = jnp.where(sel_row >= selw - N_COEF, key.astype(F32), (sel_row == block).astype(F32)).astype(BF16)
            u = _dot(qaug, jnp.concatenate([kt, ext], axis=0))
            if new_tokens:
                token = lax.broadcasted_iota(jnp.int32, (rows, KV_TILE), 0) & (DEC_ROWS - 1)
                u = jnp.where(lax.broadcasted_iota(jnp.int32, (rows, KV_TILE), 1) <= token, u, NEG_BIG)
            us.append(u)
            betas.append(slope_ref[:, 0:1] * jnp.asarray((n - n_last) * KV_TILE, jnp.int32).astype(F32))
        m = m_ref[...]
        m_new = m
        for u, beta in zip(us, betas):
            m_new = jnp.maximum(m_new, jnp.max(u, axis=1, keepdims=True) + beta)
        alpha = jnp.exp(m - m_new)
        l = alpha * l_ref[...]
        acc = alpha * acc_ref[...]
        for u, beta, (_, vt, _) in zip(us, betas, tiles):
            p = jnp.exp(u - (m_new - beta))
            l = l + jnp.sum(p, axis=1, keepdims=True)
            acc = acc + _dot_nt(p.astype(BF16), vt)
        m_ref[...] = m_new
        l_ref[...] = l
        acc_ref[...] = acc

    def page_tile(refs, j):
        return jnp.concatenate([refs[2 * j][...].reshape(w, PAGE_SIZE), refs[2 * j + 1][...].reshape(w, PAGE_SIZE)],
                               axis=1).astype(BF16)

    @pl.when(step < n_steps)
    def _():
        attend([(page_tile(k_refs, j), page_tile(v_refs, j), step * DEC_TILES + j) for j in range(DEC_TILES)], False)

    @pl.when(step == n_steps)
    def _():
        attend([(kt_ref[0].astype(BF16), vt_ref[0].astype(BF16), n_last)], True)
        o = acc_ref[...] / l_ref[...]
        if nsa:
            o = part_ref[0] + g1_ref[0] * o
        o_ref[0] = o


def _decode(q_rows, s_rows, slope_rows, pages, k_idx, v_idx, tails, page_table, extra, name):
    b, rows, w = q_rows.shape
    selw = s_rows.shape[2]
    n_pages = page_table.shape[1]
    n_last = n_pages * PAGE_SIZE // KV_TILE
    assert n_last % DEC_TILES == 0
    n_pg = 2 * DEC_TILES
    heads = pages.shape[2]
    per_b = lambda width: pl.BlockSpec((1, rows, width), lambda bi, n, pt: (bi, 0, 0))

    def page(comp, k):
        return pl.BlockSpec((1, 1, heads, HEAD_DIM, PAGE_SIZE),
                            lambda bi, n, pt: (pt[bi, jnp.minimum(n_pg * n + k, n_pages - 1)], comp, 0, 0, 0))

    tail = lambda comp: pl.BlockSpec((1, w, KV_TILE), lambda bi, n, pt: (bi, comp, 0))
    grid_spec = pltpu.PrefetchScalarGridSpec(
        num_scalar_prefetch=1,
        grid=(b, n_last // DEC_TILES + 1),
        in_specs=[per_b(w), per_b(selw), pl.BlockSpec(slope_rows.shape, lambda bi, n, pt: (0, 0))]
        + [page(k_idx, k) for k in range(n_pg)] + [page(v_idx, k) for k in range(n_pg)]
        + [tail(k_idx), tail(v_idx)] + [per_b(w) for _ in extra],
        out_specs=per_b(w),
        scratch_shapes=[pltpu.VMEM((rows, 1), F32), pltpu.VMEM((rows, 1), F32), pltpu.VMEM((rows, w), F32)],
    )
    return pl.pallas_call(
        functools.partial(_decode_body, n_last=n_last, nsa=bool(extra)),
        grid_spec=grid_spec,
        out_shape=jax.ShapeDtypeStruct((b, rows, w), F32),
        compiler_params=_params(("parallel", "arbitrary")),
        name=name,
    )(page_table, q_rows, s_rows, slope_rows, *([pages] * (2 * n_pg)), tails, tails, *extra)


KMEAN_BLOCKS = 8


def _paged_kmean_body(pt_ref, *refs):
    *page_refs, km_ref = refs
    step = pl.program_id(1)
    w = km_ref.shape[1]

    @pl.when(step == 0)
    def _():
        km_ref[...] = jnp.zeros(km_ref.shape, F32)

    lane = lax.broadcasted_iota(jnp.int32, (w, SLOT), 1)
    avg = jnp.full((PAGE_SIZE, SLOT), 1.0 / MOBA_BLOCK, BF16)
    km = km_ref[0]
    for j in range(KMEAN_BLOCKS):
        mean = jnp.zeros((w, SLOT), F32)
        for r in page_refs[2 * j:2 * j + 2]:
            hi, lo = _split_bf16(r[...].reshape(w, PAGE_SIZE))
            mean = mean + _dot(hi, avg) + _dot(lo, avg)
        km = jnp.where(lane == step * KMEAN_BLOCKS + j, mean, km)
    km_ref[0] = km


def _paged_kmean(pages, page_table):
    b, n_pages = page_table.shape
    heads = pages.shape[2]
    w = heads * HEAD_DIM
    n_past = n_pages * PAGE_SIZE // MOBA_BLOCK
    assert n_past <= SLOT and n_past % KMEAN_BLOCKS == 0
    n_pg = 2 * KMEAN_BLOCKS

    def page(k):
        return pl.BlockSpec((1, 1, heads, HEAD_DIM, PAGE_SIZE),
                            lambda bi, n, pt: (pt[bi, n_pg * n + k], 0, 0, 0, 0))

    grid_spec = pltpu.PrefetchScalarGridSpec(
        num_scalar_prefetch=1,
        grid=(b, n_past // KMEAN_BLOCKS),
        in_specs=[page(k) for k in range(n_pg)],
        out_specs=pl.BlockSpec((1, w, SLOT), lambda bi, n, pt: (bi, 0, 0)),
    )
    return pl.pallas_call(
        _paged_kmean_body,
        grid_spec=grid_spec,
        out_shape=jax.ShapeDtypeStruct((b, w, SLOT), F32),
        compiler_params=_params(("parallel", "arbitrary")),
        name="paged_kmean",
    )(page_table, *([pages] * n_pg))


def _cmp1_body(rawk_ref, rawv_ref, pos_ref, w_ref, u_ref, *, rows):
    nc = rows // CMP_STRIDE
    acc = [jnp.zeros((nc, SLOT), F32) for _ in range(4)]
    for r in range(CMP_STRIDE):
        for kv, raw_ref in enumerate((rawk_ref, rawv_ref)):
            xs = raw_ref[0, pl.ds(r, nc, stride=CMP_STRIDE), :]
            for half in range(2):
                rr = half * CMP_STRIDE + r
                xp = (xs + pos_ref[rr:rr + 1, kv * SLOT:(kv + 1) * SLOT]).astype(BF16)
                acc[2 * kv + half] = acc[2 * kv + half] + _dot(xp, w_ref[kv, half, r])
    u_ref[0] = jnp.concatenate(acc, axis=1)


def _cmp1(raw, pos_e, w1e, rows):
    b, l_pad, _ = raw.shape
    nc = rows // CMP_STRIDE
    return pl.pallas_call(
        functools.partial(_cmp1_body, rows=rows),
        grid=(b, l_pad // rows),
        in_specs=[
            pl.BlockSpec((1, rows, SLOT), lambda bi, i: (bi, i, 0)),
            pl.BlockSpec((1, rows, SLOT), lambda bi, i: (bi, i, 1)),
            pl.BlockSpec(pos_e.shape, lambda bi, i: (0, 0)),
            pl.BlockSpec(w1e.shape, lambda bi, i: (0, 0, 0, 0, 0)),
        ],
        out_specs=pl.BlockSpec((1, nc, 4 * SLOT), lambda bi, i: (bi, i, 0)),
        out_shape=jax.ShapeDtypeStruct((b, l_pad // CMP_STRIDE, 4 * SLOT), F32),
        compiler_params=_params(("parallel", "parallel")),
        name="cmp1",
    )(raw, raw, pos_e, w1e)


def _cmp2_body(ua_ref, ub_ref, w2_ref, seg_ref, gain_ref, kc_ref, vc_ref):
    ua = ua_ref[0]
    ub = ub_ref[0]
    pre_k = ua[:, 0:SLOT] + ub[:, SLOT:2 * SLOT]
    pre_v = ua[:, 2 * SLOT:3 * SLOT] + ub[:, 3 * SLOT:4 * SLOT]
    ck = _dot(jax.nn.gelu(pre_k).astype(BF16), w2_ref[0])
    cv = _dot(jax.nn.gelu(pre_v).astype(BF16), w2_ref[1])
    kc_ref[0] = _head_rms(ck, seg_ref[...], gain_ref[...]).astype(BF16)
    vc_ref[0] = cv.astype(BF16)


def _cmp2(ua, ub, w2e, seg, gain3):
    b, ncp, _ = ua.shape
    blk = pl.BlockSpec((1, ncp, 4 * SLOT), lambda bi: (bi, 0, 0))
    out = pl.BlockSpec((1, ncp, SLOT), lambda bi: (bi, 0, 0))
    return pl.pallas_call(
        _cmp2_body,
        grid=(b,),
        in_specs=[blk, blk, pl.BlockSpec(w2e.shape, lambda bi: (0, 0, 0)),
                  pl.BlockSpec(seg.shape, lambda bi: (0, 0)), pl.BlockSpec(gain3.shape, lambda bi: (0, 0))],
        out_specs=[out, out],
        out_shape=[jax.ShapeDtypeStruct((b, ncp, SLOT), BF16)] * 2,
        compiler_params=_params(("parallel",)),
        name="cmp2",
    )(ua, ub, w2e, seg, gain3)


def _nsa_sel_body(sl_ref, q_ref, g_ref, kc_ref, vc_ref, ov_ref, wk_ref, wv_ref, part_ref, g1_ref, selb_ref, *,
                  pos0, tq, n_cmp, pw0, lw_valid):
    t0 = pos0 + pl.program_id(1) * tq
    q = q_ref[0]
    gs = g_ref[0]
    ncp = kc_ref.shape[1]
    nsp = ov_ref.shape[1]
    n_wtiles = wk_ref.shape[1] // KV_TILE
    t = t0 + lax.broadcasted_iota(jnp.int32, (tq, 1), 0)
    t3 = jnp.concatenate([t, t, t], axis=0)
    lane = lax.broadcasted_iota(jnp.int32, (tq, SLOT), 1)
    mcol = lax.broadcasted_iota(jnp.int32, (tq, nsp), 1)
    mcol_f = mcol.astype(F32)
    col = lax.broadcasted_iota(jnp.int32, (1, KV_TILE), 1)
    own = _div_pow2(t, SLC_BLK)

    parts, g1s, selbs = [], [], []
    for g in range(2):
        q3 = jnp.concatenate([q[:, (3 * g + r) * SLOT:(3 * g + r + 1) * SLOT] for r in range(3)], axis=0)
        slope3 = jnp.concatenate([jnp.full((tq, 1), sl_ref[3 * g + r], F32) for r in range(3)], axis=0)

        cend = lax.broadcasted_iota(jnp.int32, (1, ncp), 1) * CMP_STRIDE + (CMP_BLK - 1)
        dist = t3 - cend
        s = _dot_nt(q3, kc_ref[0]) - slope3 * dist.astype(F32)
        mask = (dist >= 0) & (cend < n_cmp * CMP_STRIDE + (CMP_BLK - 1))
        p = _masked_softmax_rows(s, mask)
        o_cmp = _dot(p.astype(BF16), vc_ref[0])
        p_hi, p_lo = _split_bf16(p[0:tq] + p[tq:2 * tq] + p[2 * tq:3 * tq])
        imp = _dot(p_hi, ov_ref[...]) + _dot(p_lo, ov_ref[...])

        forced = (mcol == 0) | (mcol == own) | (mcol == own - 1)
        score = jnp.where(forced, FORCE_SCORE, jnp.where(mcol <= own, imp, -jnp.inf))
        sel = _topk_select(score, mcol_f, SLC_TOPK, jnp.zeros((tq, nsp), jnp.bool_))
        selbs.append(jnp.where(sel, 0.0, NEG_BIG).astype(BF16))

        wb = (t0 - pw0) // KV_TILE - 2
        s_parts, v_parts, m_parts = [], [], []
        for j in range(3):
            tile = wb + j
            off = pl.multiple_of(jnp.clip(tile, 0, n_wtiles - 1) * KV_TILE, KV_TILE)
            ridx = tile * KV_TILE + col
            dist = t3 - (pw0 + ridx)
            s_parts.append(_dot_nt(q3, wk_ref[0, pl.ds(off, KV_TILE), :]) - slope3 * dist.astype(F32))
            m_parts.append((ridx >= 0) & (ridx < lw_valid) & (dist >= 0) & (dist <= WINDOW))
            v_parts.append(wv_ref[0, pl.ds(off, KV_TILE), :])
        p = _masked_softmax_rows(jnp.concatenate(s_parts, axis=1), jnp.concatenate(m_parts, axis=1))
        o_win = sum(_dot(p[:, j * KV_TILE:(j + 1) * KV_TILE].astype(BF16), v_parts[j]) for j in range(3))

        in_group = (lane >= HEAD_DIM) if g else (lane < HEAD_DIM)
        for r in range(3):
            hd = 3 * g + r
            rows = slice(r * tq, (r + 1) * tq)
            o = gs[:, 3 * hd:3 * hd + 1] * o_cmp[rows] + gs[:, 3 * hd + 2:3 * hd + 3] * o_win[rows]
            parts.append(jnp.where(in_group, o, 0.0))
            g1s.append(jnp.where(in_group, gs[:, 3 * hd + 1:3 * hd + 2], 0.0))
    part_ref[0] = jnp.concatenate(parts, axis=1)
    g1_ref[0] = jnp.concatenate(g1s, axis=1)
    selb_ref[0] = jnp.concatenate(selbs, axis=1)


def _nsa_sel(qcx, gsig, kc, vc, ov, winb, slopes, pos0, tq, n_cmp, pw0, lw_valid):
    b, nq, wq = qcx.shape
    lw = winb.shape[1]
    ncp = kc.shape[1]
    nsp = ov.shape[1]
    grid_spec = pltpu.PrefetchScalarGridSpec(
        num_scalar_prefetch=1,
        grid=(b, nq // tq),
        in_specs=[
            pl.BlockSpec((1, tq, wq), lambda bi, i, sl: (bi, i, 0)),
            pl.BlockSpec((1, tq, SLOT), lambda bi, i, sl: (bi, i, 0)),
            pl.BlockSpec((1, ncp, SLOT), lambda bi, i, sl: (bi, 0, 0)),
            pl.BlockSpec((1, ncp, SLOT), lambda bi, i, sl: (bi, 0, 0)),
            pl.BlockSpec(ov.shape, lambda bi, i, sl: (0, 0)),
            pl.BlockSpec((1, lw, SLOT), lambda bi, i, sl: (bi, 0, 0)),
            pl.BlockSpec((1, lw, SLOT), lambda bi, i, sl: (bi, 0, 1)),
        ],
        out_specs=[pl.BlockSpec((1, tq, wq), lambda bi, i, sl: (bi, i, 0)),
                   pl.BlockSpec((1, tq, wq), lambda bi, i, sl: (bi, i, 0)),
                   pl.BlockSpec((1, tq, 2 * nsp), lambda bi, i, sl: (bi, i, 0))],
    )
    return pl.pallas_call(
        functools.partial(_nsa_sel_body, pos0=pos0, tq=tq, n_cmp=n_cmp, pw0=pw0, lw_valid=lw_valid),
        grid_spec=grid_spec,
        out_shape=[jax.ShapeDtypeStruct((b, nq, wq), F32), jax.ShapeDtypeStruct((b, nq, wq), F32),
                   jax.ShapeDtypeStruct((b, nq, 2 * nsp), BF16)],
        compiler_params=_params(("parallel", "arbitrary")),
        name="nsa_sel",
    )(slopes, qcx, gsig, kc, vc, ov, winb, winb)


def _conv_body(cin_ref, cw_ref, cb_ref, lg_ref, lb_ref, y_ref, *, tm):
    off = pl.multiple_of(pl.program_id(1) * tm, 8)
    win = cin_ref[0, pl.ds(off, tm + CONV_PAD), :]
    acc = jnp.zeros((tm, win.shape[1]), F32)
    for w in range(CONV_W):
        lo = w + CONV_PAD - (CONV_W - 1)
        acc = acc + cw_ref[w:w + 1, :] * win[lo:lo + tm, :]
    acc = acc + cb_ref[...]
    mu = jnp.mean(acc, axis=-1, keepdims=True)
    cen = acc - mu
    var = jnp.mean(cen * cen, axis=-1, keepdims=True)
    yn = cen * lax.rsqrt(var + EPS) * lg_ref[...] + lb_ref[...]
    y_ref[0] = (yn * _sigmoid(yn)).astype(BF16)


def _conv(cin, cw, cb, lg, lb, tm):
    b, rows, ch = cin.shape
    t_pad = rows - CONV_PAD
    full2 = lambda a: pl.BlockSpec(a.shape, lambda bi, i: (0, 0))
    return pl.pallas_call(
        functools.partial(_conv_body, tm=tm),
        grid=(b, t_pad // tm),
        in_specs=[pl.BlockSpec((1, rows, ch), lambda bi, i: (bi, 0, 0)), full2(cw), full2(cb), full2(lg), full2(lb)],
        out_specs=pl.BlockSpec((1, tm, ch), lambda bi, i: (bi, i, 0)),
        out_shape=jax.ShapeDtypeStruct((b, t_pad, ch), BF16),
        compiler_params=_params(("parallel", "arbitrary")),
        name="conv",
    )(cin, cw, cb, lg, lb)


def _ffn_body(x_ref, oa_ref, y_ref, oc0_ref, oc1_ref, wa_ref, wy_ref, wc0_ref, wc1_ref, g2_ref, wu_ref, wd_ref,
              out_ref):
    x1 = (x_ref[...] + _dot(oa_ref[...], wa_ref[...]) + _dot(y_ref[...], wy_ref[...])
          + _dot(oc0_ref[...], wc0_ref[...]) + _dot(oc1_ref[...], wc1_ref[...]))
    ms = jnp.mean(x1 * x1, axis=-1, keepdims=True)
    h2 = (x1 * lax.rsqrt(ms + EPS) * g2_ref[...]).astype(BF16)
    u = jnp.maximum(_dot(h2, wu_ref[...]), 0.0)
    out_ref[...] = x1 + _dot((u * u).astype(BF16), wd_ref[...])


def _ffn(x2d, acts, weights, tm):
    m, d = x2d.shape
    row = lambda a: pl.BlockSpec((tm, a.shape[1]), lambda i: (i, 0))
    const = lambda a: pl.BlockSpec(a.shape, lambda i: (0, 0), pipeline_mode=pl.Buffered(1))
    return pl.pallas_call(
        _ffn_body,
        grid=(m // tm,),
        in_specs=[row(x2d)] + [row(a) for a in acts] + [const(w) for w in weights],
        out_specs=pl.BlockSpec((tm, d), lambda i: (i, 0)),
        out_shape=jax.ShapeDtypeStruct((m, d), F32),
        compiler_params=_params(("parallel",)),
        name="out_ffn",
    )(x2d, *acts, *weights)


GATHER_PAGES = 8


def _gather_rows_body(pt_ref, *refs, n_past):
    *page_refs, tail_ref, out_ref = refs
    w = out_ref.shape[2]
    blk = jnp.concatenate([r[...].reshape(w, PAGE_SIZE).T for r in page_refs], axis=0)
    out_ref[0] = jnp.where(pl.program_id(1) < n_past, blk, tail_ref[0])


def _gather_rows(pages, n_comp, tail, page_table):
    bsz, n_pages = page_table.shape
    heads = pages.shape[2]
    w = n_comp * heads * HEAD_DIM
    assert n_pages % GATHER_PAGES == 0 and w == tail.shape[2]
    n_past = n_pages // GATHER_PAGES
    rows = GATHER_PAGES * PAGE_SIZE

    def page(k):
        return pl.BlockSpec((1, n_comp, heads, HEAD_DIM, PAGE_SIZE),
                            lambda bi, n, pt: (pt[bi, jnp.minimum(GATHER_PAGES * n + k, n_pages - 1)], 0, 0, 0, 0))

    grid_spec = pltpu.PrefetchScalarGridSpec(
        num_scalar_prefetch=1,
        grid=(bsz, n_past + 1),
        in_specs=[page(k) for k in range(GATHER_PAGES)] + [pl.BlockSpec((1, rows, w), lambda bi, n, pt: (bi, 0, 0))],
        out_specs=pl.BlockSpec((1, rows, w), lambda bi, n, pt: (bi, n, 0)),
    )
    return pl.pallas_call(
        functools.partial(_gather_rows_body, n_past=n_past),
        grid_spec=grid_spec,
        out_shape=jax.ShapeDtypeStruct((bsz, (n_past + 1) * rows, w), F32),
        compiler_params=_params(("parallel", "arbitrary")),
        name="gather_rows",
    )(page_table, *([pages] * GATHER_PAGES), tail)


def _layer_consts(lp):
    (ln1_g, w_in, qk_gain, cmp_pos, cmp_w1, cmp_w2, conv_w, conv_b, conv_ln_g, conv_ln_b, w_out, ln2_g, w_up,
     w_down) = lp
    d = w_in.shape[0]
    z64 = jnp.zeros((d, HEAD_DIM), F32)
    qc = w_in[:, 1664:2048].reshape(d, 6, HEAD_DIM)
    qc_slots = [jnp.concatenate([qc[:, h], z64] if h < 3 else [z64, qc[:, h]], axis=1) for h in range(6)]
    gate = jnp.pad(w_in[:, 2816:2834], ((0, 0), (0, SLOT - 18)))
    w_re = jnp.concatenate([w_in[:, 0:1664]] + qc_slots + [w_in[:, 2048:2816], gate], axis=1).astype(BF16)

    ones = lambda n: jnp.ones((n,), F32)
    g = qk_gain
    gain_row = jnp.concatenate([
        jnp.tile(g[0], 6) * SCALE, jnp.tile(g[1], 6), ones(C_QC - C_VA),
        jnp.tile(g[2], 12) * SCALE, ones(2 * SLOT), jnp.tile(g[4], 2), ones(SLOT),
        jnp.tile(g[5], 2), ones(SLOT), ones(SLOT)])[None, :]
    seg = jnp.asarray(np.kron(np.eye(2, dtype=np.float32), np.full((HEAD_DIM, HEAD_DIM), 1.0 / HEAD_DIM, np.float32)),
                      BF16)

    eye2 = jnp.eye(2, dtype=F32)
    w1 = cmp_w1.reshape(2, 2, CMP_STRIDE, HEAD_DIM, HEAD_DIM)
    w1e = jnp.einsum("ab,khrdj->khradbj", eye2, w1).reshape(2, 2, CMP_STRIDE, SLOT, SLOT).astype(BF16)
    w2e = jnp.einsum("ab,kdj->kadbj", eye2, cmp_w2).reshape(2, SLOT, SLOT).astype(BF16)
    pos_e = jnp.concatenate([jnp.tile(cmp_pos[0], (1, 2)), jnp.tile(cmp_pos[1], (1, 2))], axis=1)
    gain3 = jnp.tile(g[3], 2)[None, :]

    wo = w_out
    wa = wo[0:384].astype(BF16)
    wy = wo[384:640].astype(BF16)
    wc_rows = wo[640:1024].reshape(6, HEAD_DIM, d)
    zr = jnp.zeros((HEAD_DIM, d), F32)
    wc = jnp.concatenate([jnp.concatenate([wc_rows[h], zr] if h < 3 else [zr, wc_rows[h]], axis=0)
                          for h in range(6)], axis=0).astype(BF16)
    ffn_w = (wa, wy, wc[:3 * SLOT], wc[3 * SLOT:], ln2_g[None, :], w_up.astype(BF16), w_down.astype(BF16))
    return dict(ln1=ln1_g[None, :], w_re=w_re, gain_row=gain_row, seg=seg, w1e=w1e, w2e=w2e, pos_e=pos_e,
                gain3=gain3, conv_w=conv_w, conv_b=conv_b[None, :], conv_lg=conv_ln_g[None, :],
                conv_lb=conv_ln_b[None, :], ffn_w=ffn_w)


def _overlap_matrix(ncp, nsp, n_cmp, n_slc):
    n = np.arange(ncp)[:, None]
    m = np.arange(nsp)[None, :]
    ov = ((m == (n * CMP_STRIDE) // SLC_BLK) | (m == (n * CMP_STRIDE + CMP_BLK - 1) // SLC_BLK))
    ov = ov & (n < n_cmp) & (m < n_slc)
    return jnp.asarray(ov.astype(np.float32), BF16)


def _round_up(x, m):
    return -(-x // m) * m


def _pick_rows(l_pad):
    for rows in (8704, 8192, 4096, 2048, 1280, 1024, 768, 512, 256):
        if l_pad % rows == 0:
            return rows
    raise ValueError(l_pad)


def _head_rows(a, t_rows):
    b = a.shape[0]
    w = a.shape[2] // N_HEADS
    return a.reshape(b, t_rows, N_HEADS, w).transpose(0, 2, 1, 3).reshape(b, N_HEADS * t_rows, w)


def _with_coefs(s_rows, sl_tab, n_blocks):
    selw = s_rows.shape[2]
    assert n_blocks <= selw - N_COEF
    coef = np.zeros((N_HEADS * DEC_ROWS, selw), np.float32)
    coef[:, selw - N_COEF:] = np.repeat(sl_tab[:, :N_COEF], DEC_ROWS, axis=0)
    lane = np.arange(selw)[None, :] >= selw - N_COEF
    return jnp.where(jnp.asarray(lane)[None], jnp.asarray(coef, BF16)[None], s_rows)


def _layer(x, c, past, tabs):
    b, t, d = x.shape
    m = b * t
    tm = 256 if m % 256 == 0 else m
    sl_np = tabs["sl_np"]
    qa, kva, kvab, glu, qcx, kvc, slcb, win, winb, gsig, kmean_t = _proj(
        x.reshape(m, d), c["ln1"], c["w_re"], c["seg"], c["gain_row"], tm)
    r3 = lambda a: a.reshape(b, t, a.shape[-1])
    qa, kva, kvab, glu, qcx, kvc, slcb, win, winb, gsig = map(r3, (qa, kva, kvab, glu, qcx, kvc, slcb, win, winb, gsig))
    decode = past is not None

    if not decode:
        past_len = 0
        tq = KV_TILE
        assert t % KV_TILE == 0
        l_pad = t
        nb = t // MOBA_BLOCK
        kmean = kmean_t[:, 0, :].reshape(b, nb, 384)
        raw = kvc
        win_all = winb
        lw_valid = t
        pw0 = 0
        q_a, q_c, g_c = qa, qcx, gsig
        new_win_state = win[:, -min(WINDOW, t):]
        conv_in = jnp.pad(glu, ((0, 0), (CONV_PAD, 0), (0, 0)))
        new_conv = glu[:, -(CONV_W - 1):]
        conv_tm = 256
    else:
        page_table = past["page_table"]
        past_len = page_table.shape[1] * PAGE_SIZE
        assert past_len % KV_TILE == 0 and t <= DEC_ROWS
        tq = DEC_ROWS
        l_pad = past_len + KV_TILE
        nb = l_pad // MOBA_BLOCK
        to_tail = lambda a: jnp.pad(a, ((0, 0), (0, KV_TILE - t), (0, 0))).transpose(0, 2, 1)
        tail_a, tail_c = to_tail(kva), to_tail(kvc)
        kmean = _paged_kmean(past["moba_t"], page_table).transpose(0, 2, 1)[:, :nb]
        raw_tail = jnp.pad(kvc[:, :, :2 * SLOT], ((0, 0), (0, GATHER_PAGES * PAGE_SIZE - t), (0, 0)))
        raw = _gather_rows(past["nsa_t"], 2, raw_tail, page_table)
        win_buf = past["win"]
        nbuf = win_buf.shape[1]
        lw_valid = nbuf + t
        win_cat = jnp.concatenate([win_buf, win], axis=1)
        win_all = jnp.pad(win_cat, ((0, 0), (0, _round_up(lw_valid, KV_TILE) - lw_valid), (0, 0))).astype(BF16)
        pw0 = past_len - nbuf
        padq = lambda a: jnp.pad(a, ((0, 0), (0, tq - t), (0, 0)))
        q_a, q_c, g_c = padq(qa), padq(qcx), padq(gsig)
        new_win_state = win_cat[:, -nbuf:]
        conv_cat = jnp.concatenate([past["conv"], glu], axis=1)
        conv_in = jnp.pad(conv_cat, ((0, 0), (CONV_PAD - (CONV_W - 1), 8 - t), (0, 0)))
        new_conv = conv_cat[:, -(CONV_W - 1):]
        conv_tm = 8

    n_cmp = l_pad // CMP_STRIDE - 1
    n_slc = l_pad // SLC_BLK
    ncp = _round_up(n_cmp, SLOT)
    nsp = _round_up(n_slc + (N_COEF if decode else 0), SLOT)
    u = _cmp1(raw, c["pos_e"], c["w1e"], _pick_rows(raw.shape[1]))
    ua = jnp.pad(u[:, :n_cmp], ((0, 0), (0, ncp - n_cmp), (0, 0)))
    ub = jnp.pad(u[:, 1:n_cmp + 1], ((0, 0), (0, ncp - n_cmp), (0, 0)))
    kc, vc = _cmp2(ua, ub, c["w2e"], c["seg"], c["gain3"])
    ov = _overlap_matrix(ncp, nsp, n_cmp, n_slc)
    part, g1x, selb_c = _nsa_sel(q_c, g_c, kc, vc, ov, win_all, tabs["slopes"], past_len, tq, n_cmp, pw0, lw_valid)

    kmean = jnp.pad(kmean, ((0, 0), (0, SLOT - nb), (0, 0)))
    selb_a = _moba_gate(q_a, kmean, past_len, tq)

    if not decode:
        o_a = _flash_moba(q_a, selb_a, kvab, tabs["heads_a"], tabs["sl_tab"])
        oc0, oc1 = _flash_nsa(q_c, selb_c, slcb, part, g1x, tabs["heads_c"], tabs["sl_tab"])
    else:
        slope_rows =jnp.asarray(np.repeat(sl_np[:, N_COEF:], DEC_ROWS, axis=0) * np.ones((1, SLOT), np.float32))
        head_of_lane = np.arange(N_HEADS * HEAD_DIM)[None, :] // HEAD_DIM == np.arange(N_HEADS)[:, None]
        q_rows = jnp.where(jnp.asarray(head_of_lane)[None, :, None, :], q_a[:, None], jnp.zeros((), BF16))
        q_rows = q_rows.reshape(b, N_HEADS * DEC_ROWS, q_a.shape[2])
        s_rows = _with_coefs(_head_rows(selb_a, DEC_ROWS), sl_np, nb)
        o_rows = _decode(q_rows, s_rows, slope_rows, past["moba_t"], 0, 1, tail_a, page_table, (), "decode_moba")
        o_a = jnp.concatenate([o_rows[:, h * DEC_ROWS:h * DEC_ROWS + t, h * HEAD_DIM:(h + 1) * HEAD_DIM]
                               for h in range(N_HEADS)], axis=2).astype(BF16)
        sel_g = selb_c.reshape(b, DEC_ROWS, 2, nsp)
        s_rows = jnp.concatenate([sel_g[:, :, h // 3] for h in range(N_HEADS)], axis=1)
        s_rows = _with_coefs(s_rows, sl_np, n_slc)
        o_rows = _decode(_head_rows(q_c, DEC_ROWS), s_rows, slope_rows, past["nsa_t"], 2, 3, tail_c, page_table,
                         (_head_rows(part, DEC_ROWS), _head_rows(g1x, DEC_ROWS)), "decode_nsa")
        o_c = o_rows.reshape(b, N_HEADS, DEC_ROWS, SLOT).transpose(0, 2, 1, 3).reshape(b, DEC_ROWS, N_HEADS * SLOT)
        o_c = o_c[:, :t].astype(BF16)
        oc0, oc1 = o_c[:, :, :3 * SLOT], o_c[:, :, 3 * SLOT:]

    y = _conv(conv_in, c["conv_w"], c["conv_b"], c["conv_lg"], c["conv_lb"], conv_tm)[:, :t]

    f2 = lambda a: a.reshape(m, a.shape[-1])
    x_new = _ffn(x.reshape(m, d), (f2(o_a), f2(y), f2(oc0), f2(oc1)), c["ffn_w"], tm)
    return x_new.reshape(b, t, d), kva, kvc, new_win_state, new_conv


def _tables():
    sl_np = _slope_table(N_HEADS)
    return dict(sl_np=sl_np, sl_tab=jnp.asarray(sl_np), slopes=jnp.asarray(sl_np[:, N_COEF]),
                heads_a=jnp.asarray(np.array([[0, 1], [2, 3], [4, 5]], np.int32)),
                heads_c=jnp.asarray(np.array([[0, 3], [1, 4], [2, 5]], np.int32)))


def kernel(x_prompt, x_sample, cache_moba_kv, cache_nsa_kv, cache_win_kv, cache_conv, page_table, ln1_g, w_in,
           qk_gain, cmp_pos, cmp_w1, cmp_w2, conv_w, conv_b, conv_ln_g, conv_ln_b, w_out, ln2_g, w_up, w_down):
    depth = w_in.shape[0]
    bp, tp, _ = x_prompt.shape
    bs, ts, _ = x_sample.shape
    n_pool = cache_moba_kv.shape[1]
    tabs = _tables()
    yp, ys = x_prompt, x_sample
    outs = [[] for _ in range(8)]
    for l in range(depth):
        c = _layer_consts((ln1_g[l], w_in[l], qk_gain[l], cmp_pos[l], cmp_w1[l], cmp_w2[l], conv_w[l], conv_b[l],
                           conv_ln_g[l], conv_ln_b[l], w_out[l], ln2_g[l], w_up[l], w_down[l]))
        yp, a, cc, w, cv = _layer(yp, c, None, tabs)
        outs[0].append(a.reshape(bp, tp, 2, 6, HEAD_DIM))
        outs[2].append(cc.reshape(bp, tp, 4, 2, HEAD_DIM))
        outs[4].append(w.reshape(bp, w.shape[1], 2, 2, HEAD_DIM))
        outs[6].append(cv)
        past = dict(page_table=page_table + l * n_pool,
                    moba_t=cache_moba_kv.transpose(0, 1, 3, 4, 5, 2).reshape(depth * n_pool, 2, 6, HEAD_DIM, PAGE_SIZE),
                    nsa_t=cache_nsa_kv.transpose(0, 1, 3, 4, 5, 2).reshape(depth * n_pool, 4, 2, HEAD_DIM, PAGE_SIZE),
                    win=cache_win_kv[l].reshape(bs, cache_win_kv.shape[2], 256),
                    conv=cache_conv[l])
        ys, a, cc, w, cv = _layer(ys, c, past, tabs)
        outs[1].append(a.reshape(bs, ts, 2, 6, HEAD_DIM))
        outs[3].append(cc.reshape(bs, ts, 4, 2, HEAD_DIM))
        outs[5].append(w.reshape(bs, w.shape[1], 2, 2, HEAD_DIM))
        outs[7].append(cv)
    return (yp, ys) + tuple(jnp.stack(o) for o in outs)
```

```python
import functools

import numpy as np
import jax
import jax.numpy as jnp
from jax import lax
from jax.experimental import pallas as pl
from jax.experimental.pallas import tpu as pltpu

F32 = jnp.float32
BF16 = jnp.bfloat16

HEAD_DIM = 64
SLOT = 2 * HEAD_DIM
N_HEADS = 6
PAGE_SIZE = 128
MOBA_BLOCK = 256
MOBA_TOPK = 3
CMP_STRIDE = 16
CMP_BLK = 32
SLC_BLK = 64
SLC_TOPK = 16
WINDOW = 512
CONV_W = 31
CONV_PAD = 32
EPS = 1e-6
FORCE_SCORE = 1e4
SCALE = HEAD_DIM ** -0.5
NEG_BIG = -(2.0 ** 100)
M_INIT = -(2.0 ** 99)
KV_TILE = 256
DEC_ROWS = 16
N_COEF = 3
VMEM_LIMIT = 56 * 1024 * 1024

C_QA, C_KA, C_VA, C_GLU, C_QC, C_KVC, C_WIN, C_GATE, C_END = 0, 384, 768, 1152, 1664, 2432, 2944, 3200, 3328


def _alibi_slopes(n):
    return np.array([2.0 ** (-8.0 * (i + 1) / n) for i in range(n)], dtype=np.float32)


def _slope_table(n):
    sl = _alibi_slopes(n)
    terms, rest = [], sl.copy()
    for _ in range(N_COEF):
        term = rest.astype(BF16).astype(np.float32)
        terms.append(term)
        rest = rest - term
    assert not rest.any()
    return np.stack(terms + [sl], axis=1)


def _dot(a, b):
    return jnp.dot(a, b, preferred_element_type=F32)


def _dot_nt(a, b):
    return lax.dot_general(a, b, (((1,), (1,)), ((), ())), preferred_element_type=F32)


def _split_bf16(x):
    hi = x.astype(BF16)
    lo = (x - hi.astype(F32)).astype(BF16)
    return hi, lo


def _div_pow2(x, d):
    return lax.shift_right_logical(x, jnp.int32(int(d).bit_length() - 1))


def _sigmoid(x):
    return 1.0 / (1.0 + jnp.exp(-x))


def _params(sem, vmem=VMEM_LIMIT):
    return pltpu.CompilerParams(dimension_semantics=sem, vmem_limit_bytes=vmem)


def _head_rms(zs, seg, gain):
    cols = []
    for c in range(zs.shape[1] // SLOT):
        zz = zs[:, c * SLOT:(c + 1) * SLOT]
        hi, lo = _split_bf16(zz * zz)
        cols.append(_dot(hi, seg) + _dot(lo, seg))
    ms = cols[0] if len(cols) == 1 else jnp.concatenate(cols, axis=1)
    return zs * lax.rsqrt(ms + EPS) * gain


def _proj_body(x_ref, ln_ref, w_ref, seg_ref, gain_ref,
               qa_ref, kva_ref, kvab_ref, glu_ref, qcx_ref, kvc_ref, slcb_ref, win_ref, winb_ref,
               gsig_ref, kmean_ref):
    x = x_ref[...]
    ms = jnp.mean(x * x, axis=-1, keepdims=True)
    h = (x * lax.rsqrt(ms + EPS) * ln_ref[...]).astype(BF16)
    z = _dot(h, w_ref[...])
    seg = seg_ref[...]

    def normed(lo, hi):
        return _head_rms(z[:, lo:hi], seg, gain_ref[:, lo:hi])

    qa_ref[...] = normed(C_QA, C_KA).astype(BF16)
    ka = normed(C_KA, C_VA)
    kva = jnp.concatenate([ka, z[:, C_VA:C_GLU]], axis=1)
    kva_ref[...] = kva
    kvab_ref[...] = kva.astype(BF16)
    kmean_ref[0] = jnp.broadcast_to(jnp.sum(ka, axis=0, keepdims=True) * (1.0 / ka.shape[0]), (8, ka.shape[1]))
    half = (C_QC - C_GLU) // 2
    glu_ref[...] = z[:, C_GLU:C_GLU + half] * _sigmoid(z[:, C_GLU + half:C_QC])
    qcx_ref[...] = normed(C_QC, C_KVC).astype(BF16)
    slck = normed(C_KVC + 2 * SLOT, C_KVC + 3 * SLOT)
    kvc = jnp.concatenate([z[:, C_KVC:C_KVC + 2 * SLOT], slck, z[:, C_KVC + 3 * SLOT:C_WIN]], axis=1)
    kvc_ref[...] = kvc
    slcb_ref[...] = kvc[:, 2 * SLOT:].astype(BF16)
    wink = normed(C_WIN, C_WIN + SLOT)
    win = jnp.concatenate([wink, z[:, C_WIN + SLOT:C_GATE]], axis=1)
    win_ref[...] = win
    winb_ref[...] = win.astype(BF16)
    gsig_ref[...] = _sigmoid(z[:, C_GATE:C_END])


def _proj(x2d, ln_g, w_re, seg, gain_row, tm):
    m, d = x2d.shape
    nt = m // tm
    row = lambda w: pl.BlockSpec((tm, w), lambda i: (i, 0))
    full = lambda a: pl.BlockSpec(a.shape, lambda i: (0,) * a.ndim)
    outs = [(384, BF16), (768, F32), (768, BF16), (256, F32), (768, BF16), (512, F32), (256, BF16),
            (256, F32), (256, BF16), (128, F32)]
    return pl.pallas_call(
        _proj_body,
        grid=(nt,),
        in_specs=[row(d), full(ln_g), full(w_re), full(seg), full(gain_row)],
        out_specs=[row(w) for w, _ in outs] + [pl.BlockSpec((1, 8, 384), lambda i: (i, 0, 0))],
        out_shape=[jax.ShapeDtypeStruct((m, w), dt) for w, dt in outs]
        + [jax.ShapeDtypeStruct((nt, 8, 384), F32)],
        compiler_params=_params(("parallel",)),
        name="proj",
    )(x2d, ln_g, w_re, seg, gain_row)


def _topk_select(score, lane_f, k, sel):
    for _ in range(k):
        mx = jnp.max(score, axis=1, keepdims=True)
        idx = jnp.min(jnp.where(score == mx, lane_f, 1e9), axis=1, keepdims=True)
        hit = lane_f == idx
        sel = sel | (hit & (mx > -jnp.inf))
        score = jnp.where(hit, -jnp.inf, score)
    return sel


def _masked_softmax_rows(s, mask):
    sm = jnp.where(mask, s, NEG_BIG)
    m = jnp.maximum(jnp.max(sm, axis=1, keepdims=True), M_INIT)
    e = jnp.where(mask, jnp.exp(sm - m), 0.0)
    l = jnp.sum(e, axis=1, keepdims=True)
    return e / jnp.where(l > 0.0, l, 1.0)


def _moba_gate_body(q_ref, km_ref, s_ref, *, pos0, tq):
    t0 = pos0 + pl.program_id(1) * tq
    lane = lax.broadcasted_iota(jnp.int32, (tq, SLOT), 1)
    lane_f = lane.astype(F32)
    own = _div_pow2(t0 + lax.broadcasted_iota(jnp.int32, (tq, 1), 0), MOBA_BLOCK)
    for h in range(N_HEADS):
        cols = slice((h // 2) * SLOT, (h // 2 + 1) * SLOT)
        q = q_ref[0, :, cols]
        mine = (lane >= HEAD_DIM) if h % 2 else (lane < HEAD_DIM)
        qh = jnp.where(mine, q, jnp.zeros_like(q))
        km_hi, km_lo = _split_bf16(km_ref[0, :, cols])
        gate = _dot_nt(qh, km_hi) + _dot_nt(qh, km_lo)
        gate = jnp.where(lane < own, gate, -jnp.inf)
        sel = _topk_select(gate, lane_f, MOBA_TOPK, lane == own)
        s_ref[0, :, h * SLOT:(h + 1) * SLOT] = jnp.where(sel, 0.0, NEG_BIG).astype(BF16)


def _moba_gate(q, kmean, pos0, tq):
    b, nq, wa = q.shape
    return pl.pallas_call(
        functools.partial(_moba_gate_body, pos0=pos0, tq=tq),
        grid=(b, nq // tq),
        in_specs=[pl.BlockSpec((1, tq, wa), lambda bi, i: (bi, i, 0)),
                  pl.BlockSpec((1, SLOT, wa), lambda bi, i: (bi, 0, 0))],
        out_specs=pl.BlockSpec((1, tq, N_HEADS * SLOT), lambda bi, i: (bi, i, 0)),
        out_shape=jax.ShapeDtypeStruct((b, nq, N_HEADS * SLOT), BF16),
        compiler_params=_params(("parallel", "arbitrary")),
        name="moba_gate",
    )(q, kmean)


def _flash_body(hd_ref, sl_ref, qa_ref, qb_ref, sa_ref, sb_ref, k_ref, v_ref, *rest, tq, nsa):
    if nsa:
        pa_ref, pb_ref, ga_ref, gb_ref, oa_ref, ob_ref, acc_ref = rest
    else:
        o_ref, acc_ref = rest
    pair = pl.program_id(1)
    i = pl.program_id(2)
    selw = sa_ref.shape[2]
    lane = lax.broadcasted_iota(jnp.int32, (tq, SLOT), 1)
    lower = lane < HEAD_DIM
    lane_k = lax.broadcasted_iota(jnp.int32, (KV_TILE, SLOT), 1)
    klower = lane_k < HEAD_DIM
    row_kf = lax.broadcasted_iota(jnp.int32, (KV_TILE, SLOT), 0).astype(F32)
    lane_s = lax.broadcasted_iota(jnp.int32, (KV_TILE, selw), 1)
    blk_of_row = _div_pow2(lax.broadcasted_iota(jnp.int32, (KV_TILE, selw), 0), SLC_BLK)
    causal = lax.broadcasted_iota(jnp.int32, (tq, KV_TILE), 1) <= lax.broadcasted_iota(jnp.int32, (tq, KV_TILE), 0)
    ones = jnp.ones((KV_TILE, SLOT), BF16)

    qaug, slope, kpos, mine_k = [], [], [], []
    for x, (q_ref, s_ref) in enumerate(((qa_ref, sa_ref), (qb_ref, sb_ref))):
        h = hd_ref[pair, x]
        base = 0 if x else HEAD_DIM
        coef = jnp.zeros((tq, SLOT), F32)
        for j in range(N_COEF):
            coef = jnp.where(lane == base + j, sl_ref[h, j], coef)
        mine = (lane >= HEAD_DIM) if x else lower
        qx = jnp.where(mine, q_ref[0].astype(F32), coef).astype(BF16)
        qaug.append(jnp.concatenate([qx, s_ref[0]], axis=1))
        slope.append(sl_ref[h, N_COEF])
        in_coef = (lane_k >= base) & (lane_k < base + N_COEF)
        kpos.append(jnp.where(in_coef, row_kf, 0.0).astype(BF16))
        mine_k.append((lane_k >= HEAD_DIM) if x else klower)
        acc_ref[x] = jnp.zeros((tq, SLOT), F32)

    last_tile = k_ref.shape[1] // KV_TILE - 1

    def tile_offset(n):
        return pl.multiple_of(jnp.minimum(n, last_tile) * KV_TILE, KV_TILE)

    def scores(n):
        k = k_ref[0, pl.ds(tile_offset(n), KV_TILE), :]
        block = n * (KV_TILE // SLC_BLK) + blk_of_row if nsa else n
        onehot = (lane_s == block).astype(BF16)
        return tuple(_dot_nt(qaug[x], jnp.concatenate([jnp.where(mine_k[x], k, kpos[x]), onehot], axis=1))
                     for x in range(2))

    def consume(tiles, us, ms):
        vs = [v_ref[0, pl.ds(tile_offset(n), KV_TILE), :] for n in tiles]
        keeps = [(n < i) | ((n == i) & causal) for n in tiles]
        out = []
        for x in range(2):
            ue = [jnp.where(keep, u[x], NEG_BIG) for keep, u in zip(keeps, us)]
            betas = [slope[x] * ((n - i) * KV_TILE).astype(F32) for n in tiles]
            m_new = ms[x]
            for u, beta in zip(ue, betas):
                m_new = jnp.maximum(m_new, jnp.max(u, axis=1, keepdims=True) + beta)
            acc = jnp.exp(ms[x] - m_new) * acc_ref[x]
            for u, beta, v in zip(ue, betas, vs):
                p = jnp.exp(u - (m_new - beta))
                acc = acc + _dot(p.astype(BF16), jnp.where(mine_k[x], v, ones))
            acc_ref[x] = acc
            out.append(m_new)
        return tuple(out)

    def body(nn, carry):
        ms, us = carry
        us_next = (scores(2 * nn + 2), scores(2 * nn + 3))
        return consume((2 * nn, 2 * nn + 1), us, ms), us_next

    m0 = jnp.full((tq, 1), M_INIT, F32)
    lax.fori_loop(0, (i + 2) // 2, body, ((m0, m0), (scores(jnp.int32(0)), scores(jnp.int32(1)))))

    acc_a = acc_ref[0]
    acc_b = acc_ref[1]
    o_a = acc_a / acc_a[:, HEAD_DIM:HEAD_DIM + 1]
    o_b = acc_b / acc_b[:, 0:1]
    if nsa:
        oa_ref[0] = (pa_ref[0] + ga_ref[0] * o_a).astype(BF16)
        ob_ref[0] = (pb_ref[0] + gb_ref[0] * o_b).astype(BF16)
    else:
        o_ref[0] = jnp.where(lower, o_a, o_b).astype(BF16)


def _flash_moba(q, selb, kvb, heads, sl_tab):
    b, t, wa = q.shape
    npair = wa // SLOT
    tq = KV_TILE
    slot = lambda f: pl.BlockSpec((1, tq, SLOT), lambda bi, p, i, hd, sl: (bi, i, f(p)))
    kv = lambda f: pl.BlockSpec((1, t, SLOT), lambda bi, p, i, hd, sl: (bi, 0, f(p)))
    grid_spec = pltpu.PrefetchScalarGridSpec(
        num_scalar_prefetch=2,
        grid=(b, npair, t // tq),
        in_specs=[slot(lambda p: p), slot(lambda p: p), slot(lambda p: 2 * p), slot(lambda p: 2 * p + 1),
                  kv(lambda p: p), kv(lambda p: npair + p)],
        out_specs=slot(lambda p: p),
        scratch_shapes=[pltpu.VMEM((2, tq, SLOT), F32)],
    )
    return pl.pallas_call(
        functools.partial(_flash_body, tq=tq, nsa=False),
        grid_spec=grid_spec,
        out_shape=jax.ShapeDtypeStruct((b, t, wa), BF16),
        compiler_params=_params(("parallel", "parallel", "arbitrary")),
        name="flash_moba",
    )(heads, sl_tab, q, q, selb, selb, kvb, kvb)


def _flash_nsa(qcx, selb, slcb, part, g1x, heads, sl_tab):
    b, t, wq = qcx.shape
    npair = wq // SLOT // 2
    nsp = selb.shape[2] // 2
    tq = KV_TILE
    slot = lambda f: pl.BlockSpec((1, tq, SLOT), lambda bi, p, i, hd, sl: (bi, i, f(p)))
    sel = lambda g: pl.BlockSpec((1, tq, nsp), lambda bi, p, i, hd, sl: (bi, i, g))
    kv = lambda c: pl.BlockSpec((1, t, SLOT), lambda bi, p, i, hd, sl: (bi, 0, c))
    lo, hi = (lambda p: p), (lambda p: npair + p)
    grid_spec = pltpu.PrefetchScalarGridSpec(
        num_scalar_prefetch=2,
        grid=(b, npair, t // tq),
        in_specs=[slot(lo), slot(hi), sel(0), sel(1), kv(0), kv(1), slot(lo), slot(hi), slot(lo), slot(hi)],
        out_specs=[slot(lo), slot(lo)],
        scratch_shapes=[pltpu.VMEM((2, tq, SLOT), F32)],
    )
    return pl.pallas_call(
        functools.partial(_flash_body, tq=tq, nsa=True),
        grid_spec=grid_spec,
        out_shape=[jax.ShapeDtypeStruct((b, t, npair * SLOT), BF16)] * 2,
        compiler_params=_params(("parallel", "parallel", "arbitrary")),
        name="flash_nsa",
    )(heads, sl_tab, qcx, qcx, selb, selb, slcb, slcb, part, part, g1x, g1x)


DEC_TILES = 4


def _decode_body(pt_ref, q_ref, s_ref, slope_ref, *rest, n_last, nsa):
    n_pg = 2 * DEC_TILES
    k_refs, v_refs = rest[:n_pg], rest[n_pg:2 * n_pg]
    kt_ref, vt_ref = rest[2 * n_pg:2 * n_pg + 2]
    if nsa:
        part_ref, g1_ref, o_ref, m_ref, l_ref, acc_ref = rest[2 * n_pg + 2:]
    else:
        o_ref, m_ref, l_ref, acc_ref = rest[2 * n_pg + 2:]
    step = pl.program_id(1)
    n_steps = n_last // DEC_TILES
    rows = q_ref.shape[1]
    w = q_ref.shape[2]
    selw = s_ref.shape[2]

    @pl.when(step == 0)
    def _():
        m_ref[...] = jnp.full(m_ref.shape, M_INIT, F32)
        l_ref[...] = jnp.zeros(l_ref.shape, F32)
        acc_ref[...] = jnp.zeros(acc_ref.shape, F32)

    def attend(tiles, new_tokens):
        sel_row = lax.broadcasted_iota(jnp.int32, (selw, KV_TILE), 0)
        key = lax.broadcasted_iota(jnp.int32, (selw, KV_TILE), 1)
        qaug = jnp.concatenate([q_ref[0], s_ref[0]], axis=1)
        us, betas = [], []
        for kt, _, n in tiles:
            block = n * (KV_TILE // SLC_BLK) + _div_pow2(key, SLC_BLK) if nsa else n
            ext = jnp.where(sel_row >= selw - N_COEF, key.astype(F32), (sel_row == block).astype(F32)).astype(BF16)
            u = _dot(qaug, jnp.concatenate([kt, ext], axis=0))
            if new_tokens:
                token = lax.broadcasted_iota(jnp.int32, (rows, KV_TILE), 0) & (DEC_ROWS - 1)
                u = jnp.where(lax.broadcasted_iota(jnp.int32, (rows, KV_TILE), 1) <= token, u, NEG_BIG)
            us.append(u)
            betas.append(slope_ref[:, 0:1] * jnp.asarray((n - n_last) * KV_TILE, jnp.int32).astype(F32))
        m = m_ref[...]
        m_new = m
        for u, beta in zip(us, betas):
            m_new = jnp.maximum(m_new, jnp.max(u, axis=1, keepdims=True) + beta)
        alpha = jnp.exp(m - m_new)
        l = alpha * l_ref[...]
        acc = alpha * acc_ref[...]
        for u, beta, (_, vt, _) in zip(us, betas, tiles):
            p = jnp.exp(u - (m_new - beta))
            l = l + jnp.sum(p, axis=1, keepdims=True)
            acc = acc + _dot_nt(p.astype(BF16), vt)
        m_ref[...] = m_new
        l_ref[...] = l
        acc_ref[...] = acc

    def page_tile(refs, j):
        return jnp.concatenate([refs[2 * j][...].reshape(w, PAGE_SIZE), refs[2 * j + 1][...].reshape(w, PAGE_SIZE)],
                               axis=1).astype(BF16)

    @pl.when(step < n_steps)
    def _():
        attend([(page_tile(k_refs, j), page_tile(v_refs, j), step * DEC_TILES + j) for j in range(DEC_TILES)], False)

    @pl.when(step == n_steps)
    def _():
        attend([(kt_ref[0].astype(BF16), vt_ref[0].astype(BF16), n_last)], True)
        o = acc_ref[...] / l_ref[...]
        if nsa:
            o = part_ref[0] + g1_ref[0] * o
        o_ref[0] = o


def _decode(q_rows, s_rows, slope_rows, pages, k_idx, v_idx, tails, page_table, extra, name):
    b, rows, w = q_rows.shape
    selw = s_rows.shape[2]
    n_pages = page_table.shape[1]
    n_last = n_pages * PAGE_SIZE // KV_TILE
    assert n_last % DEC_TILES == 0
    n_pg = 2 * DEC_TILES
    heads = pages.shape[2]
    per_b = lambda width: pl.BlockSpec((1, rows, width), lambda bi, n, pt: (bi, 0, 0))

    def page(comp, k):
        return pl.BlockSpec((1, 1, heads, HEAD_DIM, PAGE_SIZE),
                            lambda bi, n, pt: (pt[bi, jnp.minimum(n_pg * n + k, n_pages - 1)], comp, 0, 0, 0))

    tail = lambda comp: pl.BlockSpec((1, w, KV_TILE), lambda bi, n, pt: (bi, comp, 0))
    grid_spec = pltpu.PrefetchScalarGridSpec(
        num_scalar_prefetch=1,
        grid=(b, n_last // DEC_TILES + 1),
        in_specs=[per_b(w), per_b(selw), pl.BlockSpec(slope_rows.shape, lambda bi, n, pt: (0, 0))]
        + [page(k_idx, k) for k in range(n_pg)] + [page(v_idx, k) for k in range(n_pg)]
        + [tail(k_idx), tail(v_idx)] + [per_b(w) for _ in extra],
        out_specs=per_b(w),
        scratch_shapes=[pltpu.VMEM((rows, 1), F32), pltpu.VMEM((rows, 1), F32), pltpu.VMEM((rows, w), F32)],
    )
    return pl.pallas_call(
        functools.partial(_decode_body, n_last=n_last, nsa=bool(extra)),
        grid_spec=grid_spec,
        out_shape=jax.ShapeDtypeStruct((b, rows, w), F32),
        compiler_params=_params(("parallel", "arbitrary")),
        name=name,
    )(page_table, q_rows, s_rows, slope_rows, *([pages] * (2 * n_pg)), tails, tails, *extra)


KMEAN_BLOCKS = 8


def _paged_kmean_body(pt_ref, *refs):
    *page_refs, km_ref = refs
    step = pl.program_id(1)
    w = km_ref.shape[1]

    @pl.when(step == 0)
    def _():
        km_ref[...] = jnp.zeros(km_ref.shape, F32)

    lane = lax.broadcasted_iota(jnp.int32, (w, SLOT), 1)
    avg = jnp.full((PAGE_SIZE, SLOT), 1.0 / MOBA_BLOCK, BF16)
    km = km_ref[0]
    for j in range(KMEAN_BLOCKS):
        mean = jnp.zeros((w, SLOT), F32)
        for r in page_refs[2 * j:2 * j + 2]:
            hi, lo = _split_bf16(r[...].reshape(w, PAGE_SIZE))
            mean = mean + _dot(hi, avg) + _dot(lo, avg)
        km = jnp.where(lane == step * KMEAN_BLOCKS + j, mean, km)
    km_ref[0] = km


def _paged_kmean(pages, page_table):
    b, n_pages = page_table.shape
    heads = pages.shape[2]
    w = heads * HEAD_DIM
    n_past = n_pages * PAGE_SIZE // MOBA_BLOCK
    assert n_past <= SLOT and n_past % KMEAN_BLOCKS == 0
    n_pg = 2 * KMEAN_BLOCKS

    def page(k):
        return pl.BlockSpec((1, 1, heads, HEAD_DIM, PAGE_SIZE),
                            lambda bi, n, pt: (pt[bi, n_pg * n + k], 0, 0, 0, 0))

    grid_spec = pltpu.PrefetchScalarGridSpec(
        num_scalar_prefetch=1,
        grid=(b, n_past // KMEAN_BLOCKS),
        in_specs=[page(k) for k in range(n_pg)],
        out_specs=pl.BlockSpec((1, w, SLOT), lambda bi, n, pt: (bi, 0, 0)),
    )
    return pl.pallas_call(
        _paged_kmean_body,
        grid_spec=grid_spec,
        out_shape=jax.ShapeDtypeStruct((b, w, SLOT), F32),
        compiler_params=_params(("parallel", "arbitrary")),
        name="paged_kmean",
    )(page_table, *([pages] * n_pg))


def _cmp1_body(rawk_ref, rawv_ref, pos_ref, w_ref, u_ref, *, rows):
    nc = rows // CMP_STRIDE
    acc = [jnp.zeros((nc, SLOT), F32) for _ in range(4)]
    for r in range(CMP_STRIDE):
        for kv, raw_ref in enumerate((rawk_ref, rawv_ref)):
            xs = raw_ref[0, pl.ds(r, nc, stride=CMP_STRIDE), :]
            for half in range(2):
                rr = half * CMP_STRIDE + r
                xp = (xs + pos_ref[rr:rr + 1, kv * SLOT:(kv + 1) * SLOT]).astype(BF16)
                acc[2 * kv + half] = acc[2 * kv + half] + _dot(xp, w_ref[kv, half, r])
    u_ref[0] = jnp.concatenate(acc, axis=1)


def _cmp1(raw, pos_e, w1e, rows):
    b, l_pad, _ = raw.shape
    nc = rows // CMP_STRIDE
    return pl.pallas_call(
        functools.partial(_cmp1_body, rows=rows),
        grid=(b, l_pad // rows),
        in_specs=[
            pl.BlockSpec((1, rows, SLOT), lambda bi, i: (bi, i, 0)),
            pl.BlockSpec((1, rows, SLOT), lambda bi, i: (bi, i, 1)),
            pl.BlockSpec(pos_e.shape, lambda bi, i: (0, 0)),
            pl.BlockSpec(w1e.shape, lambda bi, i: (0, 0, 0, 0, 0)),
        ],
        out_specs=pl.BlockSpec((1, nc, 4 * SLOT), lambda bi, i: (bi, i, 0)),
        out_shape=jax.ShapeDtypeStruct((b, l_pad // CMP_STRIDE, 4 * SLOT), F32),
        compiler_params=_params(("parallel", "parallel")),
        name="cmp1",
    )(raw, raw, pos_e, w1e)


def _cmp2_body(ua_ref, ub_ref, w2_ref, seg_ref, gain_ref, kc_ref, vc_ref):
    ua = ua_ref[0]
    ub = ub_ref[0]
    pre_k = ua[:, 0:SLOT] + ub[:, SLOT:2 * SLOT]
    pre_v = ua[:, 2 * SLOT:3 * SLOT] + ub[:, 3 * SLOT:4 * SLOT]
    ck = _dot(jax.nn.gelu(pre_k).astype(BF16), w2_ref[0])
    cv = _dot(jax.nn.gelu(pre_v).astype(BF16), w2_ref[1])
    kc_ref[0] = _head_rms(ck, seg_ref[...], gain_ref[...]).astype(BF16)
    vc_ref[0] = cv.astype(BF16)


def _cmp2(ua, ub, w2e, seg, gain3):
    b, ncp, _ = ua.shape
    blk = pl.BlockSpec((1, ncp, 4 * SLOT), lambda bi: (bi, 0, 0))
    out = pl.BlockSpec((1, ncp, SLOT), lambda bi: (bi, 0, 0))
    return pl.pallas_call(
        _cmp2_body,
        grid=(b,),
        in_specs=[blk, blk, pl.BlockSpec(w2e.shape, lambda bi: (0, 0, 0)),
                  pl.BlockSpec(seg.shape, lambda bi: (0, 0)), pl.BlockSpec(gain3.shape, lambda bi: (0, 0))],
        out_specs=[out, out],
        out_shape=[jax.ShapeDtypeStruct((b, ncp, SLOT), BF16)] * 2,
        compiler_params=_params(("parallel",)),
        name="cmp2",
    )(ua, ub, w2e, seg, gain3)


def _nsa_sel_body(sl_ref, q_ref, g_ref, kc_ref, vc_ref, ov_ref, wk_ref, wv_ref, part_ref, g1_ref, selb_ref, *,
                  pos0, tq, n_cmp, pw0, lw_valid):
    t0 = pos0 + pl.program_id(1) * tq
    q = q_ref[0]
    gs = g_ref[0]
    ncp = kc_ref.shape[1]
    nsp = ov_ref.shape[1]
    n_wtiles = wk_ref.shape[1] // KV_TILE
    t = t0 + lax.broadcasted_iota(jnp.int32, (tq, 1), 0)
    t3 = jnp.concatenate([t, t, t], axis=0)
    lane = lax.broadcasted_iota(jnp.int32, (tq, SLOT), 1)
    mcol = lax.broadcasted_iota(jnp.int32, (tq, nsp), 1)
    mcol_f = mcol.astype(F32)
    col = lax.broadcasted_iota(jnp.int32, (1, KV_TILE), 1)
    own = _div_pow2(t, SLC_BLK)

    parts, g1s, selbs = [], [], []
    for g in range(2):
        q3 = jnp.concatenate([q[:, (3 * g + r) * SLOT:(3 * g + r + 1) * SLOT] for r in range(3)], axis=0)
        slope3 = jnp.concatenate([jnp.full((tq, 1), sl_ref[3 * g + r], F32) for r in range(3)], axis=0)

        cend = lax.broadcasted_iota(jnp.int32, (1, ncp), 1) * CMP_STRIDE + (CMP_BLK - 1)
        dist = t3 - cend
        s = _dot_nt(q3, kc_ref[0]) - slope3 * dist.astype(F32)
        mask = (dist >= 0) & (cend < n_cmp * CMP_STRIDE + (CMP_BLK - 1))
        p = _masked_softmax_rows(s, mask)
        o_cmp = _dot(p.astype(BF16), vc_ref[0])
        p_hi, p_lo = _split_bf16(p[0:tq] + p[tq:2 * tq] + p[2 * tq:3 * tq])
        imp = _dot(p_hi, ov_ref[...]) + _dot(p_lo, ov_ref[...])

        forced = (mcol == 0) | (mcol == own) | (mcol == own - 1)
        score = jnp.where(forced, FORCE_SCORE, jnp.where(mcol <= own, imp, -jnp.inf))
        sel = _topk_select(score, mcol_f, SLC_TOPK, jnp.zeros((tq, nsp), jnp.bool_))
        selbs.append(jnp.where(sel, 0.0, NEG_BIG).astype(BF16))

        wb = (t0 - pw0) // KV_TILE - 2
        s_parts, v_parts, m_parts = [], [], []
        for j in range(3):
            tile = wb + j
            off = pl.multiple_of(jnp.clip(tile, 0, n_wtiles - 1) * KV_TILE, KV_TILE)
            ridx = tile * KV_TILE + col
            dist = t3 - (pw0 + ridx)
            s_parts.append(_dot_nt(q3, wk_ref[0, pl.ds(off, KV_TILE), :]) - slope3 * dist.astype(F32))
            m_parts.append((ridx >= 0) & (ridx < lw_valid) & (dist >= 0) & (dist <= WINDOW))
            v_parts.append(wv_ref[0, pl.ds(off, KV_TILE), :])
        p = _masked_softmax_rows(jnp.concatenate(s_parts, axis=1), jnp.concatenate(m_parts, axis=1))
        o_win = sum(_dot(p[:, j * KV_TILE:(j + 1) * KV_TILE].astype(BF16), v_parts[j]) for j in range(3))

        in_group = (lane >= HEAD_DIM) if g else (lane < HEAD_DIM)
        for r in range(3):
            hd = 3 * g + r
            rows = slice(r * tq, (r + 1) * tq)
            o = gs[:, 3 * hd:3 * hd + 1] * o_cmp[rows] + gs[:, 3 * hd + 2:3 * hd + 3] * o_win[rows]
            parts.append(jnp.where(in_group, o, 0.0))
            g1s.append(jnp.where(in_group, gs[:, 3 * hd + 1:3 * hd + 2], 0.0))
    part_ref[0] = jnp.concatenate(parts, axis=1)
    g1_ref[0] = jnp.concatenate(g1s, axis=1)
    selb_ref[0] = jnp.concatenate(selbs, axis=1)


def _nsa_sel(qcx, gsig, kc, vc, ov, winb, slopes, pos0, tq, n_cmp, pw0, lw_valid):
    b, nq, wq = qcx.shape
    lw = winb.shape[1]
    ncp = kc.shape[1]
    nsp = ov.shape[1]
    grid_spec = pltpu.PrefetchScalarGridSpec(
        num_scalar_prefetch=1,
        grid=(b, nq // tq),
        in_specs=[
            pl.BlockSpec((1, tq, wq), lambda bi, i, sl: (bi, i, 0)),
            pl.BlockSpec((1, tq, SLOT), lambda bi, i, sl: (bi, i, 0)),
            pl.BlockSpec((1, ncp, SLOT), lambda bi, i, sl: (bi, 0, 0)),
            pl.BlockSpec((1, ncp, SLOT), lambda bi, i, sl: (bi, 0, 0)),
            pl.BlockSpec(ov.shape, lambda bi, i, sl: (0, 0)),
            pl.BlockSpec((1, lw, SLOT), lambda bi, i, sl: (bi, 0, 0)),
            pl.BlockSpec((1, lw, SLOT), lambda bi, i, sl: (bi, 0, 1)),
        ],
        out_specs=[pl.BlockSpec((1, tq, wq), lambda bi, i, sl: (bi, i, 0)),
                   pl.BlockSpec((1, tq, wq), lambda bi, i, sl: (bi, i, 0)),
                   pl.BlockSpec((1, tq, 2 * nsp), lambda bi, i, sl: (bi, i, 0))],
    )
    return pl.pallas_call(
        functools.partial(_nsa_sel_body, pos0=pos0, tq=tq, n_cmp=n_cmp, pw0=pw0, lw_valid=lw_valid),
        grid_spec=grid_spec,
        out_shape=[jax.ShapeDtypeStruct((b, nq, wq), F32), jax.ShapeDtypeStruct((b, nq, wq), F32),
                   jax.ShapeDtypeStruct((b, nq, 2 * nsp), BF16)],
        compiler_params=_params(("parallel", "arbitrary")),
        name="nsa_sel",
    )(slopes, qcx, gsig, kc, vc, ov, winb, winb)


def _conv_body(cin_ref, cw_ref, cb_ref, lg_ref, lb_ref, y_ref, *, tm):
    off = pl.multiple_of(pl.program_id(1) * tm, 8)
    win = cin_ref[0, pl.ds(off, tm + CONV_PAD), :]
    acc = jnp.zeros((tm, win.shape[1]), F32)
    for w in range(CONV_W):
        lo = w + CONV_PAD - (CONV_W - 1)
        acc = acc + cw_ref[w:w + 1, :] * win[lo:lo + tm, :]
    acc = acc + cb_ref[...]
    mu = jnp.mean(acc, axis=-1, keepdims=True)
    cen = acc - mu
    var = jnp.mean(cen * cen, axis=-1, keepdims=True)
    yn = cen * lax.rsqrt(var + EPS) * lg_ref[...] + lb_ref[...]
    y_ref[0] = (yn * _sigmoid(yn)).astype(BF16)


def _conv(cin, cw, cb, lg, lb, tm):
    b, rows, ch = cin.shape
    t_pad = rows - CONV_PAD
    full2 = lambda a: pl.BlockSpec(a.shape, lambda bi, i: (0, 0))
    return pl.pallas_call(
        functools.partial(_conv_body, tm=tm),
        grid=(b, t_pad // tm),
        in_specs=[pl.BlockSpec((1, rows, ch), lambda bi, i: (bi, 0, 0)), full2(cw), full2(cb), full2(lg), full2(lb)],
        out_specs=pl.BlockSpec((1, tm, ch), lambda bi, i: (bi, i, 0)),
        out_shape=jax.ShapeDtypeStruct((b, t_pad, ch), BF16),
        compiler_params=_params(("parallel", "arbitrary")),
        name="conv",
    )(cin, cw, cb, lg, lb)


def _ffn_body(x_ref, oa_ref, y_ref, oc0_ref, oc1_ref, wa_ref, wy_ref, wc0_ref, wc1_ref, g2_ref, wu_ref, wd_ref,
              out_ref):
    x1 = (x_ref[...] + _dot(oa_ref[...], wa_ref[...]) + _dot(y_ref[...], wy_ref[...])
          + _dot(oc0_ref[...], wc0_ref[...]) + _dot(oc1_ref[...], wc1_ref[...]))
    ms = jnp.mean(x1 * x1, axis=-1, keepdims=True)
    h2 = (x1 * lax.rsqrt(ms + EPS) * g2_ref[...]).astype(BF16)
    u = jnp.maximum(_dot(h2, wu_ref[...]), 0.0)
    out_ref[...] = x1 + _dot((u * u).astype(BF16), wd_ref[...])


def _ffn(x2d, acts, weights, tm):
    m, d = x2d.shape
    row = lambda a: pl.BlockSpec((tm, a.shape[1]), lambda i: (i, 0))
    const = lambda a: pl.BlockSpec(a.shape, lambda i: (0, 0), pipeline_mode=pl.Buffered(1))
    return pl.pallas_call(
        _ffn_body,
        grid=(m // tm,),
        in_specs=[row(x2d)] + [row(a) for a in acts] + [const(w) for w in weights],
        out_specs=pl.BlockSpec((tm, d), lambda i: (i, 0)),
        out_shape=jax.ShapeDtypeStruct((m, d), F32),
        compiler_params=_params(("parallel",)),
        name="out_ffn",
    )(x2d, *acts, *weights)


GATHER_PAGES = 8


def _gather_rows_body(pt_ref, *refs, n_past):
    *page_refs, tail_ref, out_ref = refs
    w = out_ref.shape[2]
    blk = jnp.concatenate([r[...].reshape(w, PAGE_SIZE).T for r in page_refs], axis=0)
    out_ref[0] = jnp.where(pl.program_id(1) < n_past, blk, tail_ref[0])


def _gather_rows(pages, n_comp, tail, page_table):
    bsz, n_pages = page_table.shape
    heads = pages.shape[2]
    w = n_comp * heads * HEAD_DIM
    assert n_pages % GATHER_PAGES == 0 and w == tail.shape[2]
    n_past = n_pages // GATHER_PAGES
    rows = GATHER_PAGES * PAGE_SIZE

    def page(k):
        return pl.BlockSpec((1, n_comp, heads, HEAD_DIM, PAGE_SIZE),
                            lambda bi, n, pt: (pt[bi, jnp.minimum(GATHER_PAGES * n + k, n_pages - 1)], 0, 0, 0, 0))

    grid_spec = pltpu.PrefetchScalarGridSpec(
        num_scalar_prefetch=1,
        grid=(bsz, n_past + 1),
        in_specs=[page(k) for k in range(GATHER_PAGES)] + [pl.BlockSpec((1, rows, w), lambda bi, n, pt: (bi, 0, 0))],
        out_specs=pl.BlockSpec((1, rows, w), lambda bi, n, pt: (bi, n, 0)),
    )
    return pl.pallas_call(
        functools.partial(_gather_rows_body, n_past=n_past),
        grid_spec=grid_spec,
        out_shape=jax.ShapeDtypeStruct((bsz, (n_past + 1) * rows, w), F32),
        compiler_params=_params(("parallel", "arbitrary")),
        name="gather_rows",
    )(page_table, *([pages] * GATHER_PAGES), tail)


def _layer_consts(lp):
    (ln1_g, w_in, qk_gain, cmp_pos, cmp_w1, cmp_w2, conv_w, conv_b, conv_ln_g, conv_ln_b, w_out, ln2_g, w_up,
     w_down) = lp
    d = w_in.shape[0]
    z64 = jnp.zeros((d, HEAD_DIM), F32)
    qc = w_in[:, 1664:2048].reshape(d, 6, HEAD_DIM)
    qc_slots = [jnp.concatenate([qc[:, h], z64] if h < 3 else [z64, qc[:, h]], axis=1) for h in range(6)]
    gate = jnp.pad(w_in[:, 2816:2834], ((0, 0), (0, SLOT - 18)))
    w_re = jnp.concatenate([w_in[:, 0:1664]] + qc_slots + [w_in[:, 2048:2816], gate], axis=1).astype(BF16)

    ones = lambda n: jnp.ones((n,), F32)
    g = qk_gain
    gain_row = jnp.concatenate([
        jnp.tile(g[0], 6) * SCALE, jnp.tile(g[1], 6), ones(C_QC - C_VA),
        jnp.tile(g[2], 12) * SCALE, ones(2 * SLOT), jnp.tile(g[4], 2), ones(SLOT),
        jnp.tile(g[5], 2), ones(SLOT), ones(SLOT)])[None, :]
    seg = jnp.asarray(np.kron(np.eye(2, dtype=np.float32), np.full((HEAD_DIM, HEAD_DIM), 1.0 / HEAD_DIM, np.float32)),
                      BF16)

    eye2 = jnp.eye(2, dtype=F32)
    w1 = cmp_w1.reshape(2, 2, CMP_STRIDE, HEAD_DIM, HEAD_DIM)
    w1e = jnp.einsum("ab,khrdj->khradbj", eye2, w1).reshape(2, 2, CMP_STRIDE, SLOT, SLOT).astype(BF16)
    w2e = jnp.einsum("ab,kdj->kadbj", eye2, cmp_w2).reshape(2, SLOT, SLOT).astype(BF16)
    pos_e = jnp.concatenate([jnp.tile(cmp_pos[0], (1, 2)), jnp.tile(cmp_pos[1], (1, 2))], axis=1)
    gain3 = jnp.tile(g[3], 2)[None, :]

    wo = w_out
    wa = wo[0:384].astype(BF16)
    wy = wo[384:640].astype(BF16)
    wc_rows = wo[640:1024].reshape(6, HEAD_DIM, d)
    zr = jnp.zeros((HEAD_DIM, d), F32)
    wc = jnp.concatenate([jnp.concatenate([wc_rows[h], zr] if h < 3 else [zr, wc_rows[h]], axis=0)
                          for h in range(6)], axis=0).astype(BF16)
    ffn_w = (wa, wy, wc[:3 * SLOT], wc[3 * SLOT:], ln2_g[None, :], w_up.astype(BF16), w_down.astype(BF16))
    return dict(ln1=ln1_g[None, :], w_re=w_re, gain_row=gain_row, seg=seg, w1e=w1e, w2e=w2e, pos_e=pos_e,
                gain3=gain3, conv_w=conv_w, conv_b=conv_b[None, :], conv_lg=conv_ln_g[None, :],
                conv_lb=conv_ln_b[None, :], ffn_w=ffn_w)


def _overlap_matrix(ncp, nsp, n_cmp, n_slc):
    n = np.arange(ncp)[:, None]
    m = np.arange(nsp)[None, :]
    ov = ((m == (n * CMP_STRIDE) // SLC_BLK) | (m == (n * CMP_STRIDE + CMP_BLK - 1) // SLC_BLK))
    ov = ov & (n < n_cmp) & (m < n_slc)
    return jnp.asarray(ov.astype(np.float32), BF16)


def _round_up(x, m):
    return -(-x // m) * m


def _pick_rows(l_pad):
    for rows in (8704, 8192, 4096, 2048, 1280, 1024, 768, 512, 256):
        if l_pad % rows == 0:
            return rows
    raise ValueError(l_pad)


def _head_rows(a, t_rows):
    b = a.shape[0]
    w = a.shape[2] // N_HEADS
    return a.reshape(b, t_rows, N_HEADS, w).transpose(0, 2, 1, 3).reshape(b, N_HEADS * t_rows, w)


def _with_coefs(s_rows, sl_tab, n_blocks):
    selw = s_rows.shape[2]
    assert n_blocks <= selw - N_COEF
    coef = np.zeros((N_HEADS * DEC_ROWS, selw), np.float32)
    coef[:, selw - N_COEF:] = np.repeat(sl_tab[:, :N_COEF], DEC_ROWS, axis=0)
    lane = np.arange(selw)[None, :] >= selw - N_COEF
    return jnp.where(jnp.asarray(lane)[None], jnp.asarray(coef, BF16)[None], s_rows)


def _layer(x, c, past, tabs):
    b, t, d = x.shape
    m = b * t
    tm = 256 if m % 256 == 0 else m
    sl_np = tabs["sl_np"]
    qa, kva, kvab, glu, qcx, kvc, slcb, win, winb, gsig, kmean_t = _proj(
        x.reshape(m, d), c["ln1"], c["w_re"], c["seg"], c["gain_row"], tm)
    r3 = lambda a: a.reshape(b, t, a.shape[-1])
    qa, kva, kvab, glu, qcx, kvc, slcb, win, winb, gsig = map(r3, (qa, kva, kvab, glu, qcx, kvc, slcb, win, winb, gsig))
    decode = past is not None

    if not decode:
        past_len = 0
        tq = KV_TILE
        assert t % KV_TILE == 0
        l_pad = t
        nb = t // MOBA_BLOCK
        kmean = kmean_t[:, 0, :].reshape(b, nb, 384)
        raw = kvc
        win_all = winb
        lw_valid = t
        pw0 = 0
        q_a, q_c, g_c = qa, qcx, gsig
        new_win_state = win[:, -min(WINDOW, t):]
        conv_in = jnp.pad(glu, ((0, 0), (CONV_PAD, 0), (0, 0)))
        new_conv = glu[:, -(CONV_W - 1):]
        conv_tm = 256
    else:
        page_table = past["page_table"]
        past_len = page_table.shape[1] * PAGE_SIZE
        assert past_len % KV_TILE == 0 and t <= DEC_ROWS
        tq = DEC_ROWS
        l_pad = past_len + KV_TILE
        nb = l_pad // MOBA_BLOCK
        to_tail = lambda a: jnp.pad(a, ((0, 0), (0, KV_TILE - t), (0, 0))).transpose(0, 2, 1)
        tail_a, tail_c = to_tail(kva), to_tail(kvc)
        kmean = _paged_kmean(past["moba_t"], page_table).transpose(0, 2, 1)[:, :nb]
        raw_tail = jnp.pad(kvc[:, :, :2 * SLOT], ((0, 0), (0, GATHER_PAGES * PAGE_SIZE - t), (0, 0)))
        raw = _gather_rows(past["nsa_t"], 2, raw_tail, page_table)
        win_buf = past["win"]
        nbuf = win_buf.shape[1]
        lw_valid = nbuf + t
        win_cat = jnp.concatenate([win_buf, win], axis=1)
        win_all = jnp.pad(win_cat, ((0, 0), (0, _round_up(lw_valid, KV_TILE) - lw_valid), (0, 0))).astype(BF16)
        pw0 = past_len - nbuf
        padq = lambda a: jnp.pad(a, ((0, 0), (0, tq - t), (0, 0)))
        q_a, q_c, g_c = padq(qa), padq(qcx), padq(gsig)
        new_win_state = win_cat[:, -nbuf:]
        conv_cat = jnp.concatenate([past["conv"], glu], axis=1)
        conv_in = jnp.pad(conv_cat, ((0, 0), (CONV_PAD - (CONV_W - 1), 8 - t), (0, 0)))
        new_conv = conv_cat[:, -(CONV_W - 1):]
        conv_tm = 8

    n_cmp = l_pad // CMP_STRIDE - 1
    n_slc = l_pad // SLC_BLK
    ncp = _round_up(n_cmp, SLOT)
    nsp = _round_up(n_slc + (N_COEF if decode else 0), SLOT)
    u = _cmp1(raw, c["pos_e"], c["w1e"], _pick_rows(raw.shape[1]))
    ua = jnp.pad(u[:, :n_cmp], ((0, 0), (0, ncp - n_cmp), (0, 0)))
    ub = jnp.pad(u[:, 1:n_cmp + 1], ((0, 0), (0, ncp - n_cmp), (0, 0)))
    kc, vc = _cmp2(ua, ub, c["w2e"], c["seg"], c["gain3"])
    ov = _overlap_matrix(ncp, nsp, n_cmp, n_slc)
    part, g1x, selb_c = _nsa_sel(q_c, g_c, kc, vc, ov, win_all, tabs["slopes"], past_len, tq, n_cmp, pw0, lw_valid)

    kmean = jnp.pad(kmean, ((0, 0), (0, SLOT - nb), (0, 0)))
    selb_a = _moba_gate(q_a, kmean, past_len, tq)

    if not decode:
        o_a = _flash_moba(q_a, selb_a, kvab, tabs["heads_a"], tabs["sl_tab"])
        oc0, oc1 = _flash_nsa(q_c, selb_c, slcb, part, g1x, tabs["heads_c"], tabs["sl_tab"])
    else:
        slope_rows =jnp.asarray(np.repeat(sl_np[:, N_COEF:], DEC_ROWS, axis=0) * np.ones((1, SLOT), np.float32))
        head_of_lane = np.arange(N_HEADS * HEAD_DIM)[None, :] // HEAD_DIM == np.arange(N_HEADS)[:, None]
        q_rows = jnp.where(jnp.asarray(head_of_lane)[None, :, None, :], q_a[:, None], jnp.zeros((), BF16))
        q_rows = q_rows.reshape(b, N_HEADS * DEC_ROWS, q_a.shape[2])
        s_rows = _with_coefs(_head_rows(selb_a, DEC_ROWS), sl_np, nb)
        o_rows = _decode(q_rows, s_rows, slope_rows, past["moba_t"], 0, 1, tail_a, page_table, (), "decode_moba")
        o_a = jnp.concatenate([o_rows[:, h * DEC_ROWS:h * DEC_ROWS + t, h * HEAD_DIM:(h + 1) * HEAD_DIM]
                               for h in range(N_HEADS)], axis=2).astype(BF16)
        sel_g = selb_c.reshape(b, DEC_ROWS, 2, nsp)
        s_rows = jnp.concatenate([sel_g[:, :, h // 3] for h in range(N_HEADS)], axis=1)
        s_rows = _with_coefs(s_rows, sl_np, n_slc)
        o_rows = _decode(_head_rows(q_c, DEC_ROWS), s_rows, slope_rows, past["nsa_t"], 2, 3, tail_c, page_table,
                         (_head_rows(part, DEC_ROWS), _head_rows(g1x, DEC_ROWS)), "decode_nsa")
        o_c = o_rows.reshape(b, N_HEADS, DEC_ROWS, SLOT).transpose(0, 2, 1, 3).reshape(b, DEC_ROWS, N_HEADS * SLOT)
        o_c = o_c[:, :t].astype(BF16)
        oc0, oc1 = o_c[:, :, :3 * SLOT], o_c[:, :, 3 * SLOT:]

    y = _conv(conv_in, c["conv_w"], c["conv_b"], c["conv_lg"], c["conv_lb"], conv_tm)[:, :t]

    f2 = lambda a: a.reshape(m, a.shape[-1])
    x_new = _ffn(x.reshape(m, d), (f2(o_a), f2(y), f2(oc0), f2(oc1)), c["ffn_w"], tm)
    return x_new.reshape(b, t, d), kva, kvc, new_win_state, new_conv


def _tables():
    sl_np = _slope_table(N_HEADS)
    return dict(sl_np=sl_np, sl_tab=jnp.asarray(sl_np), slopes=jnp.asarray(sl_np[:, N_COEF]),
                heads_a=jnp.asarray(np.array([[0, 1], [2, 3], [4, 5]], np.int32)),
                heads_c=jnp.asarray(np.array([[0, 3], [1, 4], [2, 5]], np.int32)))


def kernel(x_prompt, x_sample, cache_moba_kv, cache_nsa_kv, cache_win_kv, cache_conv, page_table, ln1_g, w_in,
           qk_gain, cmp_pos, cmp_w1, cmp_w2, conv_w, conv_b, conv_ln_g, conv_ln_b, w_out, ln2_g, w_up, w_down):
    depth = w_in.shape[0]
    bp, tp, _ = x_prompt.shape
    bs, ts, _ = x_sample.shape
    n_pool = cache_moba_kv.shape[1]
    tabs = _tables()
    yp, ys = x_prompt, x_sample
    outs = [[] for _ in range(8)]
    for l in range(depth):
        c = _layer_consts((ln1_g[l], w_in[l], qk_gain[l], cmp_pos[l], cmp_w1[l], cmp_w2[l], conv_w[l], conv_b[l],
                           conv_ln_g[l], conv_ln_b[l], w_out[l], ln2_g[l], w_up[l], w_down[l]))
        yp, a, cc, w, cv = _layer(yp, c, None, tabs)
        outs[0].append(a.reshape(bp, tp, 2, 6, HEAD_DIM))
        outs[2].append(cc.reshape(bp, tp, 4, 2, HEAD_DIM))
        outs[4].append(w.reshape(bp, w.shape[1], 2, 2, HEAD_DIM))
        outs[6].append(cv)
        past = dict(page_table=page_table + l * n_pool,
                    moba_t=cache_moba_kv.transpose(0, 1, 3, 4, 5, 2).reshape(depth * n_pool, 2, 6, HEAD_DIM, PAGE_SIZE),
                    nsa_t=cache_nsa_kv.transpose(0, 1, 3, 4, 5, 2).reshape(depth * n_pool, 4, 2, HEAD_DIM, PAGE_SIZE),
                    win=cache_win_kv[l].reshape(bs, cache_win_kv.shape[2], 256),
                    conv=cache_conv[l])
        ys, a, cc, w, cv = _layer(ys, c, past, tabs)
        outs[1].append(a.reshape(bs, ts, 2, 6, HEAD_DIM))
        outs[3].append(cc.reshape(bs, ts, 4, 2, HEAD_DIM))
        outs[5].append(w.reshape(bs, w.shape[1], 2, 2, HEAD_DIM))
        outs[7].append(cv)
    return (yp, ys) + tuple(jnp.stack(o) for o in outs)
```
